```python
import math
import jax, jax.numpy as jnp
from jax import lax
import numpy as np

D_MODEL = 4096
BATCH = 4
SEQ = 2048
DEPTH = 4

ATT_WIDTH = D_MODEL // 2
ATT_HEAD_DIM = 128
ATT_HEADS = ATT_WIDTH // ATT_HEAD_DIM
MOBA_BLOCK = 256
MOBA_TOPK = 3
MOBA_Q_CHUNK = 16
REL_BUCKETS = 32
REL_MAX_DIST = 128

MLSTM_WIDTH = D_MODEL // 4
MLSTM_HEADS = 4
MLSTM_HEAD_DIM = MLSTM_WIDTH // MLSTM_HEADS
MLSTM_CHUNK = 64
CONV_WIDTH = 4

SGU_WIDTH = D_MODEL // 4
SGU_GROUPS = 8
SGU_GROUP_DIM = SGU_WIDTH // SGU_GROUPS
SGU_CHUNK = 128

FFN_HIDDEN = ((8 * D_MODEL // 3 + 255) // 256) * 256
N_MOD = 6

SPLIT_SIZES = (ATT_WIDTH, ATT_WIDTH, ATT_WIDTH,
               MLSTM_WIDTH, MLSTM_WIDTH, MLSTM_WIDTH, MLSTM_WIDTH,
               2 * MLSTM_HEADS,
               SGU_WIDTH, SGU_WIDTH)
IN_COLS = sum(SPLIT_SIZES)

kernel_name = "hybrid_moba_mlstm_sgu_trunk"


def rms_norm(x, g, eps=1e-6):
    x32 = x.astype(jnp.float32)
    y = x32 * lax.rsqrt(jnp.mean(x32 * x32, axis=-1, keepdims=True) + eps)
    return (y * g).astype(x.dtype)


def layer_norm(x, g, eps=1e-5):
    x32 = x.astype(jnp.float32)
    xc = x32 - jnp.mean(x32, axis=-1, keepdims=True)
    var = jnp.mean(xc * xc, axis=-1, keepdims=True)
    return (xc * lax.rsqrt(var + eps) * g).astype(x.dtype)


def rel_bucket(dist):
    max_exact = REL_BUCKETS // 2
    d = jnp.maximum(dist, 0)
    ratio = jnp.maximum(d, 1).astype(jnp.float32) / max_exact
    large = max_exact + (jnp.log(ratio) / math.log(REL_MAX_DIST / max_exact)
                         * (REL_BUCKETS - max_exact)).astype(jnp.int32)
    large = jnp.minimum(large, REL_BUCKETS - 1)
    return jnp.where(d < max_exact, d, large)


def moba_attention(q, k, v, rel_bias):
    B, T, H, dh = q.shape
    n_blocks = -(-T // MOBA_BLOCK)
    k_sel = min(MOBA_TOPK, n_blocks)
    t_pad = n_blocks * MOBA_BLOCK
    pad = ((0, 0), (0, t_pad - T), (0, 0), (0, 0))
    kp, vp = jnp.pad(k, pad), jnp.pad(v, pad)
    kbh = kp.reshape(B, n_blocks, MOBA_BLOCK, H, dh).transpose(0, 3, 1, 2, 4)
    vbh = vp.reshape(B, n_blocks, MOBA_BLOCK, H, dh).transpose(0, 3, 1, 2, 4)
    k_mean = jnp.mean(kbh.astype(jnp.float32), axis=3)
    scale = ATT_HEAD_DIM ** -0.5
    bi = jnp.arange(B)[:, None, None, None]
    hi = jnp.arange(H)[None, :, None, None]
    bias_t = rel_bias.T
    blk = jnp.arange(MOBA_BLOCK)
    n_qc = T // MOBA_Q_CHUNK
    q_chunks = q.reshape(B, n_qc, MOBA_Q_CHUNK, H, dh).transpose(1, 0, 2, 3, 4)

    def step(args):
        qc, ci = args
        start = ci * MOBA_Q_CHUNK
        t = start + jnp.arange(MOBA_Q_CHUNK)
        b_own = start // MOBA_BLOCK
        s_gate = jnp.einsum('bqhd,bhnd->bhqn', qc.astype(jnp.float32), k_mean)
        past = jnp.arange(n_blocks)[None, :] < (t // MOBA_BLOCK)[:, None]
        s_gate = jnp.where(past[None, None], s_gate, -jnp.inf)
        top_s, top_idx = lax.top_k(s_gate, k_sel)
        valid = jnp.isfinite(top_s)
        k_g = kbh[bi, hi, top_idx]
        v_g = vbh[bi, hi, top_idx]
        logit_sel = jnp.einsum('bqhd,bhqkjd->bhqkj', qc, k_g).astype(jnp.float32) * scale
        kpos = top_idx[..., None] * MOBA_BLOCK + blk
        dist_sel = t[None, None, :, None, None] - kpos
        logit_sel = logit_sel + bias_t[hi[..., None], rel_bucket(dist_sel)]
        logit_sel = jnp.where(valid[..., None], logit_sel, -jnp.inf)
        k_own = lax.dynamic_slice_in_dim(kp, b_own * MOBA_BLOCK, MOBA_BLOCK, axis=1)
        v_own = lax.dynamic_slice_in_dim(vp, b_own * MOBA_BLOCK, MOBA_BLOCK, axis=1)
        logit_own = jnp.einsum('bqhd,bjhd->bhqj', qc, k_own).astype(jnp.float32) * scale
        dist_own = t[:, None] - (b_own * MOBA_BLOCK + blk)[None, :]
        logit_own = logit_own + bias_t[:, rel_bucket(dist_own)][None]
        logit_own = jnp.where((dist_own >= 0)[None, None], logit_own, -jnp.inf)
        logits = jnp.concatenate(
            [logit_sel.reshape(B, H, MOBA_Q_CHUNK, k_sel * MOBA_BLOCK), logit_own], axis=-1)
        p = jax.nn.softmax(logits, axis=-1).astype(v.dtype)
        p_sel = p[..., :k_sel * MOBA_BLOCK].reshape(B, H, MOBA_Q_CHUNK, k_sel, MOBA_BLOCK)
        p_own = p[..., k_sel * MOBA_BLOCK:]
        return (jnp.einsum('bhqkj,bhqkjd->bqhd', p_sel, v_g)
                + jnp.einsum('bhqj,bjhd->bqhd', p_own, v_own))

    out = lax.map(step, (q_chunks, jnp.arange(n_qc)))
    return out.transpose(1, 0, 2, 3, 4).reshape(B, T, H, dh)


def causal_depthwise_conv(x, w):
    return lax.conv_general_dilated(
        x, w[:, None, :], window_strides=(1,), padding=[(CONV_WIDTH - 1, 0)],
        dimension_numbers=('NWC', 'WIO', 'NWC'), feature_group_count=x.shape[-1])


def mlstm_chunkwise(q, k, v, i_pre, f_pre):
    B, T, H, d = q.shape
    L = MLSTM_CHUNK
    nc = T // L
    f32 = jnp.float32
    to_c = lambda a: a.astype(f32).reshape(B, nc, L, H, d).transpose(1, 0, 3, 2, 4)
    to_g = lambda a: a.astype(f32).reshape(B, nc, L, H).transpose(1, 0, 3, 2)
    qc, kc, vc = to_c(q), to_c(k * d ** -0.5), to_c(v)
    ic, lfc = to_g(i_pre), to_g(jax.nn.log_sigmoid(f_pre.astype(f32)))
    causal = jnp.tril(jnp.ones((L, L), bool))

    def step(carry, xs):
        C, n, m = carry
        qx, kx, vx, ix, lf = xs
        b = jnp.cumsum(lf, axis=-1)
        a = b + m[..., None]
        Dm = jnp.where(causal, b[..., :, None] - b[..., None, :] + ix[..., None, :], -jnp.inf)
        m_t = jnp.maximum(a, jnp.max(Dm, axis=-1))
        w_inter = jnp.exp(a - m_t)
        s = jnp.einsum('bhtd,bhsd->bhts', qx, kx) * jnp.exp(Dm - m_t[..., None])
        num = (w_inter[..., None] * jnp.einsum('bhvd,bhtd->bhtv', C, qx)
               + jnp.einsum('bhts,bhsv->bhtv', s, vx))
        den = w_inter * jnp.einsum('bhd,bhtd->bht', n, qx) + jnp.sum(s, axis=-1)
        h = num / jnp.maximum(jnp.abs(den), jnp.exp(-m_t))[..., None]
        bL = b[..., -1]
        g = bL[..., None] - b + ix
        m_new = jnp.maximum(bL + m, jnp.max(g, axis=-1))
        decay = jnp.exp(bL + m - m_new)
        wk = jnp.exp(g - m_new[..., None])
        C_new = decay[..., None, None] * C + jnp.einsum('bhs,bhsv,bhsd->bhvd', wk, vx, kx)
        n_new = decay[..., None] * n + jnp.einsum('bhs,bhsd->bhd', wk, kx)
        return (C_new, n_new, m_new), h

    init = (jnp.zeros((B, H, d, d), f32), jnp.zeros((B, H, d), f32), jnp.zeros((B, H), f32))
    _, hs = lax.scan(step, init, (qc, kc, vc, ic, lfc))
    return hs.transpose(1, 0, 3, 2, 4).reshape(B, T, H, d).astype(q.dtype)


def spatial_gating(u, v, w_s, b_s, g_v):
    B, T, _ = v.shape
    nc = T // SGU_CHUNK
    v = layer_norm(v, g_v)
    vg = v.reshape(B, nc, SGU_CHUNK, SGU_GROUPS, SGU_GROUP_DIM)
    w = jnp.where(jnp.tril(jnp.ones((SGU_CHUNK, SGU_CHUNK), bool)), w_s, 0.0)
    mixed = jnp.einsum('gts,bnsgc->bntgc', w, vg) + b_s.T[None, None, :, :, None]
    return u * mixed.reshape(B, T, SGU_WIDTH)


def hybrid_mixer(h, w_in, conv_qk, b_if, g_mlstm, w_sgu, b_sgu, g_sgu, w_out, rel_bias):
    B, T, _ = h.shape
    proj = h @ w_in
    points = [int(p) for p in np.cumsum(SPLIT_SIZES)[:-1]]
    q_a, k_a, v_a, q_m, k_m, v_m, o_m, if_m, u_s, v_s = jnp.split(proj, points, axis=-1)
    heads_a = lambda a: a.reshape(B, T, ATT_HEADS, ATT_HEAD_DIM)
    y_att = moba_attention(heads_a(q_a), heads_a(k_a), heads_a(v_a), rel_bias).reshape(B, T, ATT_WIDTH)
    qk_m = jax.nn.silu(causal_depthwise_conv(jnp.concatenate([q_m, k_m], axis=-1), conv_qk))
    q_m, k_m = jnp.split(qk_m, 2, axis=-1)
    i_pre = if_m[..., :MLSTM_HEADS] + b_if[0]
    f_pre = if_m[..., MLSTM_HEADS:] + b_if[1]
    heads_m = lambda a: a.reshape(B, T, MLSTM_HEADS, MLSTM_HEAD_DIM)
    h_m = mlstm_chunkwise(heads_m(q_m), heads_m(k_m), heads_m(v_m), i_pre, f_pre)
    h_m = rms_norm(h_m, g_mlstm.reshape(MLSTM_HEADS, MLSTM_HEAD_DIM))
    y_mlstm = jax.nn.sigmoid(o_m) * h_m.reshape(B, T, MLSTM_WIDTH)
    y_sgu = spatial_gating(jax.nn.gelu(u_s), jax.nn.gelu(v_s), w_sgu, b_sgu, g_sgu)
    return jnp.concatenate([y_att, y_mlstm, y_sgu], axis=-1) @ w_out


def setup_inputs(seed: int = 0) -> dict:
    key = jax.random.key(seed)
    ks = jax.random.split(key, 24)
    nrm = lambda k, shape, s: s * jax.random.normal(k, shape, jnp.float32)
    D, F, L = D_MODEL, FFN_HIDDEN, DEPTH
    b_if = jnp.stack([nrm(ks[10], (L, MLSTM_HEADS), 0.1),
                      3.0 + nrm(ks[11], (L, MLSTM_HEADS), 0.5)], axis=1)
    return {
        "x": nrm(ks[0], (BATCH, SEQ, D), 1.0),
        "c": nrm(ks[1], (BATCH, D), 1.0),
        "w_ada": nrm(ks[2], (D, N_MOD * D), D ** -0.5),
        "b_ada": nrm(ks[3], (N_MOD * D,), 0.02),
        "ada_table": nrm(ks[4], (L, N_MOD, D), 0.1),
        "rel_bias": nrm(ks[5], (REL_BUCKETS, ATT_HEADS), 0.5),
        "g_mix_pre": 1.0 + nrm(ks[6], (L, D), 0.05),
        "g_mix_post": 1.0 + nrm(ks[7], (L, D), 0.05),
        "g_ffn_pre": 1.0 + nrm(ks[8], (L, D), 0.05),
        "g_ffn_post": 1.0 + nrm(ks[9], (L, D), 0.05),
        "w_in": nrm(ks[12], (L, D, IN_COLS), D ** -0.5),
        "conv_qk": nrm(ks[13], (L, CONV_WIDTH, 2 * MLSTM_WIDTH), CONV_WIDTH ** -0.5),
        "b_if": b_if,
        "g_mlstm": 1.0 + nrm(ks[14], (L, MLSTM_WIDTH), 0.05),
        "w_sgu": nrm(ks[15], (L, SGU_GROUPS, SGU_CHUNK, SGU_CHUNK), SGU_CHUNK ** -0.5),
        "b_sgu": 1.0 + nrm(ks[16], (L, SGU_GROUPS, SGU_CHUNK), 0.1),
        "g_sgu": 1.0 + nrm(ks[17], (L, SGU_WIDTH), 0.05),
        "w_out": nrm(ks[18], (L, D, D), D ** -0.5),
        "w_gate": nrm(ks[19], (L, D, F), D ** -0.5),
        "w_up": nrm(ks[20], (L, D, F), D ** -0.5),
        "w_down": nrm(ks[21], (L, F, D), F ** -0.5),
    }


def reference(x, c, w_ada, b_ada, ada_table, rel_bias, g_mix_pre, g_mix_post, g_ffn_pre,
              g_ffn_post, w_in, conv_qk, b_if, g_mlstm, w_sgu, b_sgu, g_sgu, w_out,
              w_gate, w_up, w_down):
    B = x.shape[0]
    base_mod = (jax.nn.silu(c) @ w_ada + b_ada).reshape(B, N_MOD, D_MODEL)
    for l in range(DEPTH):
        mod = base_mod + ada_table[l]
        shift_m, scale_m, gate_m, shift_f, scale_f, gate_f = [mod[:, j][:, None, :] for j in range(N_MOD)]
        h = rms_norm(x, g_mix_pre[l]) * (1.0 + scale_m) + shift_m
        y = hybrid_mixer(h, w_in[l], conv_qk[l], b_if[l], g_mlstm[l], w_sgu[l], b_sgu[l],
                         g_sgu[l], w_out[l], rel_bias)
        x = x + gate_m * rms_norm(y, g_mix_post[l])
        h = rms_norm(x, g_ffn_pre[l]) * (1.0 + scale_f) + shift_f
        y = (jax.nn.silu(h @ w_gate[l]) * (h @ w_up[l])) @ w_down[l]
        x = x + gate_f * rms_norm(y, g_ffn_post[l])
    return x
```

```python
import functools
import math

import numpy as np
import jax
import jax.numpy as jnp
from jax import lax
from jax.experimental import pallas as pl
from jax.experimental.pallas import tpu as pltpu

ATT_HEAD_DIM = 128
MOBA_BLOCK = 256
MOBA_TOPK = 3
REL_BUCKETS = 32
REL_MAX_DIST = 128
MLSTM_HEADS = 4
MLSTM_BLOCK = 256
CONV_WIDTH = 4
SGU_GROUPS = 8
SGU_CHUNK = 128
N_MOD = 6
RMS_EPS = 1e-6
LN_EPS = 1e-5
MASKED = -1e30

VMEM_LIMIT_BYTES = 56 * 1024 * 1024


def _cparams(n_axes):
    return pltpu.CompilerParams(dimension_semantics=("arbitrary",) * n_axes,
                                vmem_limit_bytes=VMEM_LIMIT_BYTES)


def _dot(a, b):
    return jnp.dot(a, b, preferred_element_type=jnp.float32)


def _dot_nt(a, b):
    return lax.dot_general(a, b, (((1,), (1,)), ((), ())), preferred_element_type=jnp.float32)


def _dot_tn(a, b):
    return lax.dot_general(a, b, (((0,), (0,)), ((), ())), preferred_element_type=jnp.float32)


def _silu(x):
    return x * jax.nn.sigmoid(x)


def _gelu_tanh(x):
    return 0.5 * x * (1.0 + jnp.tanh(math.sqrt(2.0 / math.pi) * (x + 0.044715 * (x * x * x))))


def _ada_kernel(c_ref, w_ref, b_ref, o_ref):
    a = _silu(c_ref[...]).astype(jnp.bfloat16)
    o_ref[...] = _dot(a, w_ref[...].astype(jnp.bfloat16)) + b_ref[...]


def _ada_base(c, w_ada, b_ada):
    b, d = c.shape
    n = w_ada.shape[1]
    rows = 16
    c_pad = jnp.zeros((rows, d), jnp.float32).at[:b].set(c)
    tn = 1024
    out = pl.pallas_call(
        _ada_kernel,
        grid=(n // tn,),
        in_specs=[pl.BlockSpec((rows, d), lambda j: (0, 0)),
                  pl.BlockSpec((d, tn), lambda j: (0, j)),
                  pl.BlockSpec((1, tn), lambda j: (0, j))],
        out_specs=pl.BlockSpec((rows, tn), lambda j: (0, j)),
        out_shape=jax.ShapeDtypeStruct((rows, n), jnp.float32),
        compiler_params=_cparams(1),
        name="ada_base",
    )(c_pad, w_ada, b_ada.reshape(1, n))
    return out[:b].reshape(b, N_MOD, n // N_MOD)


def _prenorm_kernel(shift_idx, scale_idx, x_ref, g_ref, mod_ref, tab_ref, o_ref):
    x = x_ref[0]
    y = x * lax.rsqrt(jnp.mean(x * x, axis=-1, keepdims=True) + RMS_EPS) * g_ref[...]
    scale = mod_ref[0, scale_idx:scale_idx + 1, :] + tab_ref[scale_idx:scale_idx + 1, :]
    shift = mod_ref[0, shift_idx:shift_idx + 1, :] + tab_ref[shift_idx:shift_idx + 1, :]
    o_ref[0] = (y * (1.0 + scale) + shift).astype(o_ref.dtype)


def _prenorm(x, g, base_mod, tab, shift_idx, scale_idx):
    b, t, d = x.shape
    tt = 256
    return pl.pallas_call(
        functools.partial(_prenorm_kernel, shift_idx, scale_idx),
        grid=(b, t // tt),
        in_specs=[pl.BlockSpec((1, tt, d), lambda bi, ti: (bi, ti, 0)),
                  pl.BlockSpec((1, d), lambda bi, ti: (0, 0)),
                  pl.BlockSpec((1, N_MOD, d), lambda bi, ti: (bi, 0, 0)),
                  pl.BlockSpec((N_MOD, d), lambda bi, ti: (0, 0))],
        out_specs=pl.BlockSpec((1, tt, d), lambda bi, ti: (bi, ti, 0)),
        out_shape=jax.ShapeDtypeStruct((b, t, d), jnp.bfloat16),
        compiler_params=_cparams(2),
        name="prenorm",
    )(x, g.reshape(1, d), base_mod, tab)


def _postnorm_kernel(gate_idx, x_ref, y_ref, g_ref, mod_ref, tab_ref, o_ref):
    y = y_ref[0]
    yn = y * lax.rsqrt(jnp.mean(y * y, axis=-1, keepdims=True) + RMS_EPS) * g_ref[...]
    gate = mod_ref[0, gate_idx:gate_idx + 1, :] + tab_ref[gate_idx:gate_idx + 1, :]
    o_ref[0] = x_ref[0] + gate * yn


def _postnorm_residual(x, y, g, base_mod, tab, gate_idx):
    b, t, d = x.shape
    tt = 256
    return pl.pallas_call(
        functools.partial(_postnorm_kernel, gate_idx),
        grid=(b, t // tt),
        in_specs=[pl.BlockSpec((1, tt, d), lambda bi, ti: (bi, ti, 0)),
                  pl.BlockSpec((1, tt, d), lambda bi, ti: (bi, ti, 0)),
                  pl.BlockSpec((1, d), lambda bi, ti: (0, 0)),
                  pl.BlockSpec((1, N_MOD, d), lambda bi, ti: (bi, 0, 0)),
                  pl.BlockSpec((N_MOD, d), lambda bi, ti: (0, 0))],
        out_specs=pl.BlockSpec((1, tt, d), lambda bi, ti: (bi, ti, 0)),
        out_shape=jax.ShapeDtypeStruct((b, t, d), jnp.float32),
        input_output_aliases={0: 0},
        compiler_params=_cparams(2),
        name="postnorm_residual",
    )(x, y, g.reshape(1, d), base_mod, tab)


def _mm_kernel(x_ref, w_ref, o_ref):
    o_ref[...] = _dot(x_ref[...], w_ref[...]).astype(o_ref.dtype)


def _matmul(x, w, tm, tn, out_dtype, name):
    m, k = x.shape
    n = w.shape[1]
    return pl.pallas_call(
        _mm_kernel,
        grid=(m // tm, n // tn),
        in_specs=[pl.BlockSpec((tm, k), lambda i, j: (i, 0)),
                  pl.BlockSpec((k, tn), lambda i, j: (0, j))],
        out_specs=pl.BlockSpec((tm, tn), lambda i, j: (i, j)),
        out_shape=jax.ShapeDtypeStruct((m, n), out_dtype),
        compiler_params=_cparams(2),
        name=name,
    )(x, w)


def _swiglu_kernel(x_ref, wg_ref, wu_ref, o_ref):
    x = x_ref[...]
    o_ref[...] = (_silu(_dot(x, wg_ref[...])) * _dot(x, wu_ref[...])).astype(o_ref.dtype)


def _swiglu_matmul(x, wg, wu, tm, tn):
    m, k = x.shape
    n = wg.shape[1]
    return pl.pallas_call(
        _swiglu_kernel,
        grid=(m // tm, n // tn),
        in_specs=[pl.BlockSpec((tm, k), lambda i, j: (i, 0)),
                  pl.BlockSpec((k, tn), lambda i, j: (0, j)),
                  pl.BlockSpec((k, tn), lambda i, j: (0, j))],
        out_specs=pl.BlockSpec((tm, tn), lambda i, j: (i, j)),
        out_shape=jax.ShapeDtypeStruct((m, n), jnp.bfloat16),
        compiler_params=_cparams(2),
        name="ffn_gate_up",
    )(x, wg, wu)


def _mm_kacc_kernel(x_ref, w_ref, o_ref, acc_ref):
    kk = pl.program_id(2)

    @pl.when(kk == 0)
    def _():
        acc_ref[...] = jnp.zeros_like(acc_ref)

    acc_ref[...] += _dot(x_ref[...], w_ref[...])

    @pl.when(kk == pl.num_programs(2) - 1)
    def _():
        o_ref[...] = acc_ref[...].astype(o_ref.dtype)


def _matmul_kacc(x, w, tm, tn, tk, out_dtype, name):
    m, k = x.shape
    n = w.shape[1]
    return pl.pallas_call(
        _mm_kacc_kernel,
        grid=(m // tm, n // tn, k // tk),
        in_specs=[pl.BlockSpec((tm, tk), lambda i, j, kk: (i, kk)),
                  pl.BlockSpec((tk, tn), lambda i, j, kk: (kk, j))],
        out_specs=pl.BlockSpec((tm, tn), lambda i, j, kk: (i, j)),
        out_shape=jax.ShapeDtypeStruct((m, n), out_dtype),
        scratch_shapes=[pltpu.VMEM((tm, tn), jnp.float32)],
        compiler_params=_cparams(3),
        name=name,
    )(x, w)


def _rel_bucket_starts():
    max_exact = REL_BUCKETS // 2
    d = np.arange(0, 4 * REL_MAX_DIST)
    ratio = np.maximum(d, 1).astype(np.float32) / np.float32(max_exact)
    large = max_exact + (np.log(ratio) / np.float32(math.log(REL_MAX_DIST / max_exact))
                         * np.float32(REL_BUCKETS - max_exact)).astype(np.int32)
    bucket = np.where(d < max_exact, d, np.minimum(large, REL_BUCKETS - 1))
    assert np.all(np.diff(bucket) >= 0)
    return [int(np.argmax(bucket == j)) for j in range(REL_BUCKETS)]


def _rel_tiles_kernel(starts, rel_ref, diag_ref, near_ref):
    h = pl.program_id(0)
    key = lax.broadcasted_iota(jnp.int32, (MOBA_BLOCK, MOBA_BLOCK), 0)
    qry = lax.broadcasted_iota(jnp.int32, (MOBA_BLOCK, MOBA_BLOCK), 1)
    dist = qry - key

    def bias_of(d):
        val = jnp.full(d.shape, rel_ref[0, h], jnp.float32)
        for j in range(1, REL_BUCKETS):
            val = jnp.where(d >= starts[j], rel_ref[j, h], val)
        return val

    diag_ref[0] = jnp.where(dist >= 0, bias_of(dist), MASKED)
    near_ref[0] = bias_of(dist + MOBA_BLOCK)


def _rel_bias_tiles(rel_bias):
    heads = rel_bias.shape[1]
    tile = jax.ShapeDtypeStruct((heads, MOBA_BLOCK, MOBA_BLOCK), jnp.float32)
    spec = pl.BlockSpec((1, MOBA_BLOCK, MOBA_BLOCK), lambda h: (h, 0, 0))
    return pl.pallas_call(
        functools.partial(_rel_tiles_kernel, _rel_bucket_starts()),
        grid=(heads,),
        in_specs=[pl.BlockSpec(memory_space=pltpu.SMEM)],
        out_specs=[spec, spec],
        out_shape=[tile, tile],
        compiler_params=_cparams(1),
        name="rel_bias_tiles",
    )(rel_bias)


def _moba_kernel(n_blocks, rel_ref, q_ref, k_ref, v_ref, diag_ref, near_ref, o_ref,
                 kmean_ref, far_ref, nearmask_ref, m_ref, l_ref, acc_ref):
    h = pl.program_id(1)
    i = pl.program_id(2)
    blk = MOBA_BLOCK
    scale = ATT_HEAD_DIM ** -0.5

    @pl.when(i == 0)
    def _():
        t = n_blocks * blk
        row = lax.broadcasted_iota(jnp.int32, (n_blocks, t), 0)
        col = lax.broadcasted_iota(jnp.int32, (n_blocks, t), 1)
        lo = row * blk
        avg = jnp.where((col >= lo) & (col < lo + blk), 1.0 / blk, 0.0).astype(jnp.bfloat16)
        kmean_ref[...] = _dot(avg, k_ref[0])

    q = q_ref[0]

    kmean = kmean_ref[...]
    km_hi = kmean.astype(jnp.bfloat16)
    km_lo = (kmean - km_hi.astype(jnp.float32)).astype(jnp.bfloat16)
    s_gate = _dot_nt(km_hi, q) + _dot_nt(km_lo, q)
    bid = lax.broadcasted_iota(jnp.int32, s_gate.shape, 0)
    valid = bid < i
    s_gate = jnp.where(valid, s_gate, -jnp.inf)
    rank = jnp.zeros(s_gate.shape, jnp.int32)
    for j in range(n_blocks):
        s_j = s_gate[j:j + 1, :]
        beats = (s_j > s_gate) | ((s_j == s_gate) & (j < bid))
        rank = rank + jnp.where(beats & (j < i), 1, 0)
    sel = valid & (rank < MOBA_TOPK)
    far_ref[...] = jnp.where(sel, rel_ref[REL_BUCKETS - 1, h], MASKED)
    nearmask_ref[...] = jnp.where(sel, 0.0, MASKED)

    def scores(j):
        k_j = k_ref[0, pl.ds(pl.multiple_of(j * blk, blk), blk), :]
        return _dot_nt(k_j, q) * scale

    def values(j):
        return v_ref[0, pl.ds(pl.multiple_of(j * blk, blk), blk), :]

    s = scores(i) + diag_ref[0]
    m = jnp.max(s, axis=0, keepdims=True)
    p = jnp.exp(s - m)
    m_ref[...] = m
    l_ref[...] = jnp.sum(p, axis=0, keepdims=True)
    acc_ref[...] = _dot_tn(values(i), p.astype(jnp.bfloat16))

    def online_update(j, s):
        m_old = m_ref[...]
        m_new = jnp.maximum(m_old, jnp.max(s, axis=0, keepdims=True))
        alpha = jnp.exp(m_old - m_new)
        p = jnp.exp(s - m_new)
        m_ref[...] = m_new
        l_ref[...] = alpha * l_ref[...] + jnp.sum(p, axis=0, keepdims=True)
        acc_ref[...] = alpha * acc_ref[...] + _dot_tn(values(j), p.astype(jnp.bfloat16))

    @pl.when(i >= 1)
    def _():
        j = i - 1
        online_update(j, scores(j) + near_ref[0] + nearmask_ref[pl.ds(j, 1), :])

    def far_body(j, carry):
        online_update(j, scores(j) + far_ref[pl.ds(j, 1), :])
        return carry

    lax.fori_loop(0, jnp.maximum(i - 1, 0), far_body, 0)

    out_t = acc_ref[...] / l_ref[...]
    o_ref[0] = out_t.T.astype(o_ref.dtype)


def _moba_attention(proj, rel_bias, diag_tiles, near_tiles, heads):
    b, t, _ = proj.shape
    blk, dh = MOBA_BLOCK, ATT_HEAD_DIM
    n_blocks = t // blk
    tile_spec = pl.BlockSpec((1, blk, blk), lambda bi, h, i: (h, 0, 0))
    return pl.pallas_call(
        functools.partial(_moba_kernel, n_blocks),
        grid=(b, heads, n_blocks),
        in_specs=[pl.BlockSpec(memory_space=pltpu.SMEM),
                  pl.BlockSpec((1, blk, dh), lambda bi, h, i: (bi, i, h)),
                  pl.BlockSpec((1, t, dh), lambda bi, h, i: (bi, 0, heads + h)),
                  pl.BlockSpec((1, t, dh), lambda bi, h, i: (bi, 0, 2 * heads + h)),
                  tile_spec, tile_spec],
        out_specs=pl.BlockSpec((1, blk, dh), lambda bi, h, i: (bi, i, h)),
        out_shape=jax.ShapeDtypeStruct((b, t, heads * dh), jnp.bfloat16),
        scratch_shapes=[pltpu.VMEM((n_blocks, dh), jnp.float32),
                        pltpu.VMEM((n_blocks, blk), jnp.float32),
                        pltpu.VMEM((n_blocks, blk), jnp.float32),
                        pltpu.VMEM((1, blk), jnp.float32),
                        pltpu.VMEM((1, blk), jnp.float32),
                        pltpu.VMEM((dh, blk), jnp.float32)],
        compiler_params=_cparams(3),
        name="moba_attention",
    )(rel_bias, proj, proj, proj, diag_tiles, near_tiles)


def _scan_rows(x, op, identity):
    n = x.shape[0]
    row = lax.broadcasted_iota(jnp.int32, x.shape, 0)
    s = 1
    while s < n:
        x = op(x, jnp.where(row >= s, pltpu.roll(x, s, axis=0), identity))
        s *= 2
    return x


def _mlstm_gates_kernel(if_ref, bias_ref, g_ref, mb_ref, eb_ref):
    pre = if_ref[0] + bias_ref[...]
    lanes = pre.shape[1]
    i_pre = pre
    f_pre = pltpu.roll(pre, lanes - MLSTM_HEADS, axis=1)
    log_f = -(jnp.maximum(-f_pre, 0.0) + jnp.log1p(jnp.exp(-jnp.abs(f_pre))))
    b = _scan_rows(log_f, jnp.add, 0.0)
    g = i_pre - b
    m_rel = jnp.maximum(_scan_rows(g, jnp.maximum, -jnp.inf), 0.0)
    e = jnp.exp(-(b + m_rel))
    g_t = g.T
    t = pre.shape[0]
    blk = MLSTM_BLOCK
    for hh in range(MLSTM_HEADS):
        for n in range(t // blk):
            g_ref[0, hh, n:n + 1, :] = g_t[hh:hh + 1, n * blk:(n + 1) * blk]
        mb_ref[0, hh] = jnp.broadcast_to(m_rel[:, hh:hh + 1], (t, lanes))
        eb_ref[0, hh] = jnp.broadcast_to(e[:, hh:hh + 1], (t, lanes))


def _mlstm_gates(if_pre, bias_row):
    b, t, lanes = if_pre.shape
    col = jax.ShapeDtypeStruct((b, MLSTM_HEADS, t, lanes), jnp.float32)
    col_spec = pl.BlockSpec((1, MLSTM_HEADS, t, lanes), lambda bi: (bi, 0, 0, 0))
    return pl.pallas_call(
        _mlstm_gates_kernel,
        grid=(b,),
        in_specs=[pl.BlockSpec((1, t, lanes), lambda bi: (bi, 0, 0)),
                  pl.BlockSpec((1, lanes), lambda bi: (0, 0))],
        out_specs=[pl.BlockSpec((1, MLSTM_HEADS, t // MLSTM_BLOCK, MLSTM_BLOCK), lambda bi: (bi, 0, 0, 0)),
                   col_spec, col_spec],
        out_shape=[jax.ShapeDtypeStruct((b, MLSTM_HEADS, t // MLSTM_BLOCK, MLSTM_BLOCK), jnp.float32), col, col],
        compiler_params=_cparams(1),
        name="mlstm_gates",
    )(if_pre, bias_row)


def _causal_conv_silu(x, w):
    row = lax.broadcasted_iota(jnp.int32, x.shape, 0)
    out = x * w[CONV_WIDTH - 1:CONV_WIDTH, :]
    for back in range(1, CONV_WIDTH):
        shifted = jnp.where(row >= back, pltpu.roll(x, back, axis=0), 0.0)
        out = out + shifted * w[CONV_WIDTH - 1 - back:CONV_WIDTH - back, :]
    return _silu(out)


def _mlstm_kernel(q_ref, k_ref, v_ref, o_ref, wq_ref, wk_ref, g_ref, mb_ref, eb_ref, gn_ref, y_ref,
                  qc_ref, kc_ref):
    i = pl.program_id(2)
    blk = MLSTM_BLOCK
    d = q_ref.shape[2]

    @pl.when(i == 0)
    def _():
        qc_ref[...] = _causal_conv_silu(q_ref[0].astype(jnp.float32), wq_ref[...]).astype(jnp.bfloat16)
        kc = _causal_conv_silu(k_ref[0].astype(jnp.float32), wk_ref[...]) * (d ** -0.5)
        kc_ref[...] = kc.astype(jnp.bfloat16)

    q = qc_ref[pl.ds(pl.multiple_of(i * blk, blk), blk), :]
    m_col = mb_ref[0, 0]
    m_full = jnp.concatenate([m_col] * (blk // m_col.shape[1]), axis=1)

    def weighted_scores(j, causal):
        rows = pl.ds(pl.multiple_of(j * blk, blk), blk)
        qk = _dot_nt(q, kc_ref[rows, :])
        w = jnp.exp(g_ref[0, 0, pl.ds(j, 1), :] - m_full)
        if causal:
            qi = lax.broadcasted_iota(jnp.int32, w.shape, 0)
            ki = lax.broadcasted_iota(jnp.int32, w.shape, 1)
            w = jnp.where(ki <= qi, w, 0.0)
        return qk * w, v_ref[0, rows, :]

    s, v = weighted_scores(i, True)
    num0 = _dot(s.astype(jnp.bfloat16), v)

    def body(j, carry):
        num, ssum = carry
        s, v = weighted_scores(j, False)
        return num + _dot(s.astype(jnp.bfloat16), v), ssum + s

    num, ssum = lax.fori_loop(0, i, body, (num0, s))
    den = jnp.sum(ssum, axis=-1, keepdims=True)
    hid = num / jnp.maximum(jnp.abs(den), eb_ref[0, 0][:, 0:1])
    hn = hid * lax.rsqrt(jnp.mean(hid * hid, axis=-1, keepdims=True) + RMS_EPS) * gn_ref[...]
    y_ref[0] = (jax.nn.sigmoid(o_ref[0].astype(jnp.float32)) * hn).astype(y_ref.dtype)


def _mlstm(proj, conv_w, g_rows, m_cols, e_cols, g_norm, col0):
    b, t, _ = proj.shape
    heads = MLSTM_HEADS
    d = g_norm.shape[0] // heads
    blk = MLSTM_BLOCK
    c0 = col0 // d
    lanes = m_cols.shape[3]
    full = lambda off: pl.BlockSpec((1, t, d), lambda bi, h, i: (bi, 0, c0 + off * heads + h))
    col_spec = pl.BlockSpec((1, 1, blk, lanes), lambda bi, h, i: (bi, h, i, 0))
    return pl.pallas_call(
        _mlstm_kernel,
        grid=(b, heads, t // blk),
        in_specs=[full(0), full(1), full(2),
                  pl.BlockSpec((1, blk, d), lambda bi, h, i: (bi, i, c0 + 3 * heads + h)),
                  pl.BlockSpec((CONV_WIDTH, d), lambda bi, h, i: (0, h)),
                  pl.BlockSpec((CONV_WIDTH, d), lambda bi, h, i: (0, heads + h)),
                  pl.BlockSpec((1, 1, t // blk, blk), lambda bi, h, i: (bi, h, 0, 0)),
                  col_spec, col_spec,
                  pl.BlockSpec((1, d), lambda bi, h, i: (0, h))],
        out_specs=pl.BlockSpec((1, blk, d), lambda bi, h, i: (bi, i, h)),
        out_shape=jax.ShapeDtypeStruct((b, t, heads * d), jnp.bfloat16),
        scratch_shapes=[pltpu.VMEM((t, d), jnp.bfloat16), pltpu.VMEM((t, d), jnp.bfloat16)],
        compiler_params=_cparams(3),
        name="mlstm",
    )(proj, proj, proj, proj, conv_w, conv_w, g_rows, m_cols, e_cols, g_norm.reshape(1, -1))


def _sgu_kernel(u_ref, v_ref, w_ref, bt_ref, g_ref, y_ref):
    chunk = SGU_CHUNK
    tt, width = u_ref.shape[1], u_ref.shape[2]
    gd = width // SGU_GROUPS
    v = _gelu_tanh(v_ref[0].astype(jnp.float32))
    vc = v - jnp.mean(v, axis=-1, keepdims=True)
    var = jnp.mean(vc * vc, axis=-1, keepdims=True)
    vn = (vc * lax.rsqrt(var + LN_EPS) * g_ref[...]).astype(jnp.bfloat16)
    row = lax.broadcasted_iota(jnp.int32, (chunk, chunk), 0)
    col = lax.broadcasted_iota(jnp.int32, (chunk, chunk), 1)
    for g in range(SGU_GROUPS):
        w = jnp.where(col <= row, w_ref[g], 0.0).astype(jnp.bfloat16)
        bias = bt_ref[:, g:g + 1]
        cols = slice(g * gd, (g + 1) * gd)
        for n in range(tt // chunk):
            rows = slice(n * chunk, (n + 1) * chunk)
            mixed = _dot(w, vn[rows, cols]) + bias
            u = _gelu_tanh(u_ref[0, rows, cols].astype(jnp.float32))
            y_ref[0, rows, cols] = (u * mixed).astype(y_ref.dtype)


def _sgu(proj, w_s, b_s, g_v, col0):
    b, t, _ = proj.shape
    width = g_v.shape[0]
    tt = 512
    c0 = col0 // width
    return pl.pallas_call(
        _sgu_kernel,
        grid=(b, t // tt),
        in_specs=[pl.BlockSpec((1, tt, width), lambda bi, ti: (bi, ti, c0)),
                  pl.BlockSpec((1, tt, width), lambda bi, ti: (bi, ti, c0 + 1)),
                  pl.BlockSpec(w_s.shape, lambda bi, ti: (0, 0, 0)),
                  pl.BlockSpec((SGU_CHUNK, SGU_GROUPS), lambda bi, ti: (0, 0)),
                  pl.BlockSpec((1, width), lambda bi, ti: (0, 0))],
        out_specs=pl.BlockSpec((1, tt, width), lambda bi, ti: (bi, ti, 0)),
        out_shape=jax.ShapeDtypeStruct((b, t, width), jnp.bfloat16),
        compiler_params=_cparams(2),
        name="sgu",
    )(proj, proj, w_s, b_s.T, g_v.reshape(1, width))


def kernel(x, c, w_ada, b_ada, ada_table, rel_bias, g_mix_pre, g_mix_post, g_ffn_pre, g_ffn_post,
           w_in, conv_qk, b_if, g_mlstm, w_sgu, b_sgu, g_sgu, w_out, w_gate, w_up, w_down):
    bsz, t, d = x.shape
    depth = w_in.shape[0]
    bf16 = jnp.bfloat16
    att_heads = rel_bias.shape[1]
    att_w = att_heads * ATT_HEAD_DIM
    ml_w = g_mlstm.shape[1]
    sgu_w = g_sgu.shape[1]
    n_gate = 2 * MLSTM_HEADS
    gate0 = 3 * att_w + 4 * ml_w
    ffn = w_gate.shape[2]
    ffn_pad = -(-ffn // 1024) * 1024

    base_mod = _ada_base(c, w_ada, b_ada)
    diag_tiles, near_tiles = _rel_bias_tiles(rel_bias)

    for l in range(depth):
        tab = ada_table[l]
        w_main = jnp.concatenate([w_in[l][:, :gate0], w_in[l][:, gate0 + n_gate:]], axis=1).astype(bf16)
        w_if = jnp.pad(w_in[l][:, gate0:gate0 + n_gate], ((0, 0), (0, 128 - n_gate))).astype(bf16)
        bias_row = jnp.pad(b_if[l].reshape(1, n_gate), ((0, 0), (0, 128 - n_gate)))

        h = _prenorm(x, g_mix_pre[l], base_mod, tab, 0, 1)
        h2 = h.reshape(bsz * t, d)
        proj = _matmul(h2, w_main, 1024, 1024, bf16, "in_proj").reshape(bsz, t, -1)
        if_pre = _matmul(h2, w_if, 1024, 128, jnp.float32, "gate_proj").reshape(bsz, t, 128)

        y_att = _moba_attention(proj, rel_bias, diag_tiles, near_tiles, att_heads)
        g_rows, m_cols, e_cols = _mlstm_gates(if_pre, bias_row)
        y_ml = _mlstm(proj, conv_qk[l], g_rows, m_cols, e_cols, g_mlstm[l], 3 * att_w)
        y_sgu = _sgu(proj, w_sgu[l], b_sgu[l], g_sgu[l], 3 * att_w + 4 * ml_w)
        y_cat = jnp.concatenate([y_att, y_ml, y_sgu], axis=-1).reshape(bsz * t, d)
        y = _matmul(y_cat, w_out[l].astype(bf16), 1024, 1024, jnp.float32, "out_proj")
        x = _postnorm_residual(x, y.reshape(bsz, t, d), g_mix_post[l], base_mod, tab, 2)

        pad_cols = ((0, 0), (0, ffn_pad - ffn))
        wg = jnp.pad(w_gate[l], pad_cols).astype(bf16)
        wu = jnp.pad(w_up[l], pad_cols).astype(bf16)
        wd = jnp.pad(w_down[l], ((0, ffn_pad - ffn), (0, 0))).astype(bf16)
        h = _prenorm(x, g_ffn_pre[l], base_mod, tab, 3, 4).reshape(bsz * t, d)
        act = _swiglu_matmul(h, wg, wu, 1024, 512)
        y = _matmul_kacc(act, wd, 1024, 1024, ffn_pad // 4, jnp.float32, "ffn_down")
        x = _postnorm_residual(x, y.reshape(bsz, t, d), g_ffn_post[l], base_mod, tab, 5)
    return x
```

```python
import functools
import math

import numpy as np
import jax
import jax.numpy as jnp
from jax import lax
from jax.experimental import pallas as pl
from jax.experimental.pallas import tpu as pltpu

ATT_HEAD_DIM = 128
MOBA_BLOCK = 256
MOBA_TOPK = 3
REL_BUCKETS = 32
REL_MAX_DIST = 128
MLSTM_HEADS = 4
MLSTM_BLOCK = 256
CONV_WIDTH = 4
SGU_GROUPS = 8
SGU_CHUNK = 128
N_MOD = 6
RMS_EPS = 1e-6
LN_EPS = 1e-5
MASKED = -1e30

VMEM_LIMIT_BYTES = 56 * 1024 * 1024


def _cparams(n_axes):
    return pltpu.CompilerParams(dimension_semantics=("arbitrary",) * n_axes,
                                vmem_limit_bytes=VMEM_LIMIT_BYTES)


def _dot(a, b):
    return jnp.dot(a, b, preferred_element_type=jnp.float32)


def _dot_nt(a, b):
    return lax.dot_general(a, b, (((1,), (1,)), ((), ())), preferred_element_type=jnp.float32)


def _dot_tn(a, b):
    return lax.dot_general(a, b, (((0,), (0,)), ((), ())), preferred_element_type=jnp.float32)


def _silu(x):
    return x * jax.nn.sigmoid(x)


def _gelu_tanh(x):
    return 0.5 * x * (1.0 + jnp.tanh(math.sqrt(2.0 / math.pi) * (x + 0.044715 * (x * x * x))))


def _ada_kernel(c_ref, w_ref, b_ref, o_ref):
    a = _silu(c_ref[...]).astype(jnp.bfloat16)
    o_ref[...] = _dot(a, w_ref[...].astype(jnp.bfloat16)) + b_ref[...]


def _ada_base(c, w_ada, b_ada):
    b, d = c.shape
    n = w_ada.shape[1]
    rows = 16
    c_pad = jnp.zeros((rows, d), jnp.float32).at[:b].set(c)
    tn = 1024
    out = pl.pallas_call(
        _ada_kernel,
        grid=(n // tn,),
        in_specs=[pl.BlockSpec((rows, d), lambda j: (0, 0)),
                  pl.BlockSpec((d, tn), lambda j: (0, j)),
                  pl.BlockSpec((1, tn), lambda j: (0, j))],
        out_specs=pl.BlockSpec((rows, tn), lambda j: (0, j)),
        out_shape=jax.ShapeDtypeStruct((rows, n), jnp.float32),
        compiler_params=_cparams(1),
        name="ada_base",
    )(c_pad, w_ada, b_ada.reshape(1, n))
    return out[:b].reshape(b, N_MOD, n // N_MOD)


def _prenorm_kernel(shift_idx, scale_idx, x_ref, g_ref, mod_ref, tab_ref, o_ref):
    x = x_ref[0]
    y = x * lax.rsqrt(jnp.mean(x * x, axis=-1, keepdims=True) + RMS_EPS) * g_ref[...]
    scale = mod_ref[0, scale_idx:scale_idx + 1, :] + tab_ref[scale_idx:scale_idx + 1, :]
    shift = mod_ref[0, shift_idx:shift_idx + 1, :] + tab_ref[shift_idx:shift_idx + 1, :]
    o_ref[0] = (y * (1.0 + scale) + shift).astype(o_ref.dtype)


def _prenorm(x, g, base_mod, tab, shift_idx, scale_idx):
    b, t, d = x.shape
    tt = 256
    return pl.pallas_call(
        functools.partial(_prenorm_kernel, shift_idx, scale_idx),
        grid=(b, t // tt),
        in_specs=[pl.BlockSpec((1, tt, d), lambda bi, ti: (bi, ti, 0)),
                  pl.BlockSpec((1, d), lambda bi, ti: (0, 0)),
                  pl.BlockSpec((1, N_MOD, d), lambda bi, ti: (bi, 0, 0)),
                  pl.BlockSpec((N_MOD, d), lambda bi, ti: (0, 0))],
        out_specs=pl.BlockSpec((1, tt, d), lambda bi, ti: (bi, ti, 0)),
        out_shape=jax.ShapeDtypeStruct((b, t, d), jnp.bfloat16),
        compiler_params=_cparams(2),
        name="prenorm",
    )(x, g.reshape(1, d), base_mod, tab)


def _postnorm_kernel(gate_idx, x_ref, y_ref, g_ref, mod_ref, tab_ref, o_ref):
    y = y_ref[0]
    yn = y * lax.rsqrt(jnp.mean(y * y, axis=-1, keepdims=True) + RMS_EPS) * g_ref[...]
    gate = mod_ref[0, gate_idx:gate_idx + 1, :] + tab_ref[gate_idx:gate_idx + 1, :]
    o_ref[0] = x_ref[0] + gate * yn


def _postnorm_residual(x, y, g, base_mod, tab, gate_idx):
    b, t, d = x.shape
    tt = 256
    return pl.pallas_call(
        functools.partial(_postnorm_kernel, gate_idx),
        grid=(b, t // tt),
        in_specs=[pl.BlockSpec((1, tt, d), lambda bi, ti: (bi, ti, 0)),
                  pl.BlockSpec((1, tt, d), lambda bi, ti: (bi, ti, 0)),
                  pl.BlockSpec((1, d), lambda bi, ti: (0, 0)),
                  pl.BlockSpec((1, N_MOD, d), lambda bi, ti: (bi, 0, 0)),
                  pl.BlockSpec((N_MOD, d), lambda bi, ti: (0, 0))],
        out_specs=pl.BlockSpec((1, tt, d), lambda bi, ti: (bi, ti, 0)),
        out_shape=jax.ShapeDtypeStruct((b, t, d), jnp.float32),
        input_output_aliases={0: 0},
        compiler_params=_cparams(2),
        name="postnorm_residual",
    )(x, y, g.reshape(1, d), base_mod, tab)


def _mm_kernel(x_ref, w_ref, o_ref):
    o_ref[...] = _dot(x_ref[...], w_ref[...]).astype(o_ref.dtype)


def _matmul(x, w, tm, tn, out_dtype, name):
    m, k = x.shape
    n = w.shape[1]
    return pl.pallas_call(
        _mm_kernel,
        grid=(m // tm, n // tn),
        in_specs=[pl.BlockSpec((tm, k), lambda i, j: (i, 0)),
                  pl.BlockSpec((k, tn), lambda i, j: (0, j))],
        out_specs=pl.BlockSpec((tm, tn), lambda i, j: (i, j)),
        out_shape=jax.ShapeDtypeStruct((m, n), out_dtype),
        compiler_params=_cparams(2),
        name=name,
    )(x, w)


def _cast_weight_once(w_ref, wb_ref):
    @pl.when(pl.program_id(1) == 0)
    def _():
        wb_ref[...] = w_ref[0].astype(jnp.bfloat16)


def _mm_wcast_kernel(x_ref, w_ref, o_ref, wb_ref):
    _cast_weight_once(w_ref, wb_ref)
    o_ref[...] = _dot(x_ref[...], wb_ref[...]).astype(o_ref.dtype)


def _matmul_wcast(x, w_stack, layer, n_cols, tm, tn, out_dtype, name):
    m, k = x.shape
    return pl.pallas_call(
        _mm_wcast_kernel,
        grid=(n_cols // tn, m // tm),
        in_specs=[pl.BlockSpec((tm, k), lambda j, i: (i, 0)),
                  pl.BlockSpec((1, k, tn), lambda j, i: (layer, 0, j))],
        out_specs=pl.BlockSpec((tm, tn), lambda j, i: (i, j)),
        out_shape=jax.ShapeDtypeStruct((m, n_cols), out_dtype),
        scratch_shapes=[pltpu.VMEM((k, tn), jnp.bfloat16)],
        compiler_params=_cparams(2),
        name=name,
    )(x, w_stack)


def _mm3_wcast_kernel(xa_ref, xb_ref, xc_ref, w_ref, o_ref, wb_ref):
    _cast_weight_once(w_ref, wb_ref)
    ka, kb = xa_ref.shape[1], xb_ref.shape[1]
    o_ref[...] = (_dot(xa_ref[...], wb_ref[0:ka, :]) + _dot(xb_ref[...], wb_ref[ka:ka + kb, :])
                  + _dot(xc_ref[...], wb_ref[ka + kb:, :])).astype(o_ref.dtype)


def _matmul3_wcast(xa, xb, xc, w_stack, layer, tm, tn, out_dtype, name):
    m = xa.shape[0]
    k, n = w_stack.shape[1], w_stack.shape[2]
    x_spec = lambda x: pl.BlockSpec((tm, x.shape[1]), lambda j, i: (i, 0))
    return pl.pallas_call(
        _mm3_wcast_kernel,
        grid=(n // tn, m // tm),
        in_specs=[x_spec(xa), x_spec(xb), x_spec(xc),
                  pl.BlockSpec((1, k, tn), lambda j, i: (layer, 0, j))],
        out_specs=pl.BlockSpec((tm, tn), lambda j, i: (i, j)),
        out_shape=jax.ShapeDtypeStruct((m, n), out_dtype),
        scratch_shapes=[pltpu.VMEM((k, tn), jnp.bfloat16)],
        compiler_params=_cparams(2),
        name=name,
    )(xa, xb, xc, w_stack)


def _swiglu_wcast_kernel(x_ref, wg_ref, wu_ref, o_ref, wgb_ref, wub_ref):
    _cast_weight_once(wg_ref, wgb_ref)
    _cast_weight_once(wu_ref, wub_ref)
    x = x_ref[...]
    o_ref[...] = (_silu(_dot(x, wgb_ref[...])) * _dot(x, wub_ref[...])).astype(o_ref.dtype)


def _swiglu_matmul(x, wg_stack, wu_stack, layer, tm, tn):
    m, k = x.shape
    n = wg_stack.shape[2]
    w_spec = pl.BlockSpec((1, k, tn), lambda j, i: (layer, 0, j))
    return pl.pallas_call(
        _swiglu_wcast_kernel,
        grid=(n // tn, m // tm),
        in_specs=[pl.BlockSpec((tm, k), lambda j, i: (i, 0)), w_spec, w_spec],
        out_specs=pl.BlockSpec((tm, tn), lambda j, i: (i, j)),
        out_shape=jax.ShapeDtypeStruct((m, n), jnp.bfloat16),
        scratch_shapes=[pltpu.VMEM((k, tn), jnp.bfloat16), pltpu.VMEM((k, tn), jnp.bfloat16)],
        compiler_params=_cparams(2),
        name="ffn_gate_up",
    )(x, wg_stack, wu_stack)


def _mm_kacc_kernel(x_ref, w_ref, o_ref, acc_ref):
    kk = pl.program_id(2)

    @pl.when(kk == 0)
    def _():
        acc_ref[...] = jnp.zeros_like(acc_ref)

    acc_ref[...] += _dot(x_ref[...], w_ref[...])

    @pl.when(kk == pl.num_programs(2) - 1)
    def _():
        o_ref[...] = acc_ref[...].astype(o_ref.dtype)


def _matmul_kacc(x, w, tm, tn, tk, out_dtype, name):
    m, k = x.shape
    n = w.shape[1]
    return pl.pallas_call(
        _mm_kacc_kernel,
        grid=(m // tm, n // tn, k // tk),
        in_specs=[pl.BlockSpec((tm, tk), lambda i, j, kk: (i, kk)),
                  pl.BlockSpec((tk, tn), lambda i, j, kk: (kk, j))],
        out_specs=pl.BlockSpec((tm, tn), lambda i, j, kk: (i, j)),
        out_shape=jax.ShapeDtypeStruct((m, n), out_dtype),
        scratch_shapes=[pltpu.VMEM((tm, tn), jnp.float32)],
        compiler_params=_cparams(3),
        name=name,
    )(x, w)


def _rel_bucket_starts():
    max_exact = REL_BUCKETS // 2
    d = np.arange(0, 4 * REL_MAX_DIST)
    ratio = np.maximum(d, 1).astype(np.float32) / np.float32(max_exact)
    large = max_exact + (np.log(ratio) / np.float32(math.log(REL_MAX_DIST / max_exact))
                         * np.float32(REL_BUCKETS - max_exact)).astype(np.int32)
    bucket = np.where(d < max_exact, d, np.minimum(large, REL_BUCKETS - 1))
    assert np.all(np.diff(bucket) >= 0)
    return [int(np.argmax(bucket == j)) for j in range(REL_BUCKETS)]


def _rel_tiles_kernel(starts, rel_ref, diag_ref, near_ref):
    h = pl.program_id(0)
    qry = lax.broadcasted_iota(jnp.int32, (MOBA_BLOCK, MOBA_BLOCK), 0)
    key = lax.broadcasted_iota(jnp.int32, (MOBA_BLOCK, MOBA_BLOCK), 1)
    dist = qry - key

    def bias_of(d):
        val = jnp.full(d.shape, rel_ref[0, h], jnp.float32)
        for j in range(1, REL_BUCKETS):
            val = jnp.where(d >= starts[j], rel_ref[j, h], val)
        return val

    diag_ref[0] = jnp.where(dist >= 0, bias_of(dist), MASKED)
    near_ref[0] = bias_of(dist + MOBA_BLOCK)


def _rel_bias_tiles(rel_bias):
    heads = rel_bias.shape[1]
    tile = jax.ShapeDtypeStruct((heads, MOBA_BLOCK, MOBA_BLOCK), jnp.float32)
    spec = pl.BlockSpec((1, MOBA_BLOCK, MOBA_BLOCK), lambda h: (h, 0, 0))
    return pl.pallas_call(
        functools.partial(_rel_tiles_kernel, _rel_bucket_starts()),
        grid=(heads,),
        in_specs=[pl.BlockSpec(memory_space=pltpu.SMEM)],
        out_specs=[spec, spec],
        out_shape=[tile, tile],
        compiler_params=_cparams(1),
        name="rel_bias_tiles",
    )(rel_bias)


def _moba_kernel(n_blocks, rel_ref, q_ref, k_ref, v_ref, diag_ref, near_ref, o_ref, kaug_ref):
    h = pl.program_id(1)
    blk, dh = MOBA_BLOCK, ATT_HEAD_DIM
    t = n_blocks * blk
    scale = dh ** -0.5
    far_bias = rel_ref[REL_BUCKETS - 1, h]
    k = k_ref[0]

    row = lax.broadcasted_iota(jnp.int32, (t, dh), 0)
    col = lax.broadcasted_iota(jnp.int32, (t, dh), 1)
    lo = col * blk
    kaug_ref[:, 0:dh] = k
    kaug_ref[:, dh:2 * dh] = jnp.where((row >= lo) & (row < lo + blk), 1.0, 0.0).astype(jnp.bfloat16)

    brow = lax.broadcasted_iota(jnp.int32, (n_blocks, t), 0) * blk
    bcol = lax.broadcasted_iota(jnp.int32, (n_blocks, t), 1)
    avg = jnp.where((bcol >= brow) & (bcol < brow + blk), 1.0 / blk, 0.0).astype(jnp.bfloat16)
    kmean = _dot(avg, k)
    km_hi = kmean.astype(jnp.bfloat16)
    km_lo = (kmean - km_hi.astype(jnp.float32)).astype(jnp.bfloat16)

    for i in range(n_blocks):
        rows = slice(i * blk, (i + 1) * blk)
        n_keys = (i + 1) * blk
        q = q_ref[0, rows, :]
        if i > MOBA_TOPK:
            s_gate = _dot_nt(km_hi, q) + _dot_nt(km_lo, q)
            bid = lax.broadcasted_iota(jnp.int32, s_gate.shape, 0)
            rank = jnp.zeros(s_gate.shape, jnp.int32)
            for j in range(i):
                s_j = s_gate[j:j + 1, :]
                beats = (s_j > s_gate) | ((s_j == s_gate) & (j < bid))
                rank = rank + jnp.where(beats, 1, 0)
            offs = jnp.where((bid < i) & (rank >= MOBA_TOPK), MASKED, 0.0)
            offs = jnp.concatenate([offs, jnp.zeros((dh - n_blocks, blk), jnp.float32)], axis=0)
            q_aug = jnp.concatenate([q, offs.T.astype(jnp.bfloat16)], axis=1)
            s = _dot_nt(q_aug, kaug_ref[0:n_keys, :])
        else:
            s = _dot_nt(q, k_ref[0, 0:n_keys, :])
        s = s * scale
        parts = []
        if i >= 2:
            parts.append(s[:, :(i - 1) * blk] + far_bias)
        if i >= 1:
            parts.append(s[:, (i - 1) * blk:i * blk] + near_ref[0])
        parts.append(s[:, i * blk:] + diag_ref[0])
        s = jnp.concatenate(parts, axis=1) if len(parts) > 1 else parts[0]
        p = jnp.exp(s - jnp.max(s, axis=1, keepdims=True))
        denom = jnp.sum(p, axis=1, keepdims=True)
        out = _dot(p.astype(jnp.bfloat16), v_ref[0, 0:n_keys, :]) / denom
        o_ref[0, rows, :] = out.astype(o_ref.dtype)


def _moba_attention(proj, rel_bias, diag_tiles, near_tiles, heads):
    b, t, _ = proj.shape
    blk, dh = MOBA_BLOCK, ATT_HEAD_DIM
    n_blocks = t // blk
    assert n_blocks <= dh
    tile_spec = pl.BlockSpec((1, blk, blk), lambda bi, h: (h, 0, 0))
    seq_spec = lambda off: pl.BlockSpec((1, t, dh), lambda bi, h: (bi, 0, off * heads + h))
    return pl.pallas_call(
        functools.partial(_moba_kernel, n_blocks),
        grid=(b, heads),
        in_specs=[pl.BlockSpec(memory_space=pltpu.SMEM), seq_spec(0), seq_spec(1), seq_spec(2),
                  tile_spec, tile_spec],
        out_specs=pl.BlockSpec((1, t, dh), lambda bi, h: (bi, 0, h)),
        out_shape=jax.ShapeDtypeStruct((b, t, heads * dh), jnp.bfloat16),
        scratch_shapes=[pltpu.VMEM((t, 2 * dh), jnp.bfloat16)],
        compiler_params=_cparams(2),
        name="moba_attention",
    )(rel_bias, proj, proj, proj, diag_tiles, near_tiles)


def _scan_rows(x, op, identity):
    n = x.shape[0]
    row = lax.broadcasted_iota(jnp.int32, x.shape, 0)
    s = 1
    while s < n:
        x = op(x, jnp.where(row >= s, pltpu.roll(x, s, axis=0), identity))
        s *= 2
    return x


def _mlstm_gates_kernel(if_ref, bias_ref, g_ref, mb_ref, eb_ref):
    pre = if_ref[0] + bias_ref[...]
    lanes = pre.shape[1]
    i_pre = pre
    f_pre = pltpu.roll(pre, lanes - MLSTM_HEADS, axis=1)
    log_f = -(jnp.maximum(-f_pre, 0.0) + jnp.log1p(jnp.exp(-jnp.abs(f_pre))))
    b = _scan_rows(log_f, jnp.add, 0.0)
    g = i_pre - b
    m_rel = jnp.maximum(_scan_rows(g, jnp.maximum, -jnp.inf), 0.0)
    e = jnp.exp(-(b + m_rel))
    g_t = g.T
    t = pre.shape[0]
    for hh in range(MLSTM_HEADS):
        g_ref[0, hh] = g_t[hh:hh + 1, :]
        mb_ref[0, hh] = jnp.broadcast_to(m_rel[:, hh:hh + 1], (t, lanes))
        eb_ref[0, hh] = jnp.broadcast_to(e[:, hh:hh + 1], (t, lanes))


def _mlstm_gates(if_pre, bias_row):
    b, t, lanes = if_pre.shape
    col = jax.ShapeDtypeStruct((b, MLSTM_HEADS, t, lanes), jnp.float32)
    col_spec = pl.BlockSpec((1, MLSTM_HEADS, t, lanes), lambda bi: (bi, 0, 0, 0))
    return pl.pallas_call(
        _mlstm_gates_kernel,
        grid=(b,),
        in_specs=[pl.BlockSpec((1, t, lanes), lambda bi: (bi, 0, 0)),
                  pl.BlockSpec((1, lanes), lambda bi: (0, 0))],
        out_specs=[pl.BlockSpec((1, MLSTM_HEADS, 1, t), lambda bi: (bi, 0, 0, 0)), col_spec, col_spec],
        out_shape=[jax.ShapeDtypeStruct((b, MLSTM_HEADS, 1, t), jnp.float32), col, col],
        compiler_params=_cparams(1),
        name="mlstm_gates",
    )(if_pre, bias_row)


def _causal_conv_silu(x, w):
    row = lax.broadcasted_iota(jnp.int32, x.shape, 0)
    out = x * w[CONV_WIDTH - 1:CONV_WIDTH, :]
    for back in range(1, CONV_WIDTH):
        shifted = jnp.where(row >= back, pltpu.roll(x, back, axis=0), 0.0)
        out = out + shifted * w[CONV_WIDTH - 1 - back:CONV_WIDTH - back, :]
    return _silu(out)


def _mlstm_kernel(q_ref, k_ref, v_ref, o_ref, wq_ref, wk_ref, g_ref, mb_ref, eb_ref, gn_ref, y_ref,
                  qc_ref, kc_ref):
    blk = MLSTM_BLOCK
    t, d = q_ref.shape[1], q_ref.shape[2]
    lanes = mb_ref.shape[3]
    qc_ref[...] = _causal_conv_silu(q_ref[0].astype(jnp.float32), wq_ref[...]).astype(jnp.bfloat16)
    kc = _causal_conv_silu(k_ref[0].astype(jnp.float32), wk_ref[...]) * (d ** -0.5)
    kc_ref[...] = kc.astype(jnp.bfloat16)
    qi = lax.broadcasted_iota(jnp.int32, (blk, blk), 0)
    ki = lax.broadcasted_iota(jnp.int32, (blk, blk), 1)

    for i in range(t // blk):
        rows = slice(i * blk, (i + 1) * blk)
        n_keys = (i + 1) * blk
        qk = _dot_nt(qc_ref[rows, :], kc_ref[0:n_keys, :])
        m_col = mb_ref[0, 0, rows, :]
        m_full = jnp.concatenate([m_col] * (n_keys // lanes), axis=1)
        w = jnp.exp(g_ref[0, 0, :, 0:n_keys] - m_full)
        w_own = jnp.where(ki <= qi, w[:, i * blk:], 0.0)
        w = jnp.concatenate([w[:, :i * blk], w_own], axis=1) if i else w_own
        s = qk * w
        den = jnp.sum(s, axis=-1, keepdims=True)
        num = _dot(s.astype(jnp.bfloat16), v_ref[0, 0:n_keys, :])
        hid = num / jnp.maximum(jnp.abs(den), eb_ref[0, 0, rows, 0:1])
        hn = hid * lax.rsqrt(jnp.mean(hid * hid, axis=-1, keepdims=True) + RMS_EPS) * gn_ref[...]
        y_ref[0, rows, :] = (jax.nn.sigmoid(o_ref[0, rows, :].astype(jnp.float32)) * hn).astype(y_ref.dtype)


def _mlstm(proj, conv_w, g_rows, m_cols, e_cols, g_norm, col0):
    b, t, _ = proj.shape
    heads = MLSTM_HEADS
    d = g_norm.shape[0] // heads
    c0 = col0 // d
    lanes = m_cols.shape[3]
    full = lambda off: pl.BlockSpec((1, t, d), lambda bi, h: (bi, 0, c0 + off * heads + h))
    col_spec = pl.BlockSpec((1, 1, t, lanes), lambda bi, h: (bi, h, 0, 0))
    return pl.pallas_call(
        _mlstm_kernel,
        grid=(b, heads),
        in_specs=[full(0), full(1), full(2), full(3),
                  pl.BlockSpec((CONV_WIDTH, d), lambda bi, h: (0, h)),
                  pl.BlockSpec((CONV_WIDTH, d), lambda bi, h: (0, heads + h)),
                  pl.BlockSpec((1, 1, 1, t), lambda bi, h: (bi, h, 0, 0)),
                  col_spec, col_spec,
                  pl.BlockSpec((1, d), lambda bi, h: (0, h))],
        out_specs=pl.BlockSpec((1, t, d), lambda bi, h: (bi, 0, h)),
        out_shape=jax.ShapeDtypeStruct((b, t, heads * d), jnp.bfloat16),
        scratch_shapes=[pltpu.VMEM((t, d), jnp.bfloat16), pltpu.VMEM((t, d), jnp.bfloat16)],
        compiler_params=_cparams(2),
        name="mlstm",
    )(proj, proj, proj, proj, conv_w, conv_w, g_rows, m_cols, e_cols, g_norm.reshape(1, -1))


def _sgu_kernel(u_ref, v_ref, w_ref, bt_ref, g_ref, y_ref):
    chunk = SGU_CHUNK
    tt, width = u_ref.shape[1], u_ref.shape[2]
    gd = width // SGU_GROUPS
    v = _gelu_tanh(v_ref[0].astype(jnp.float32))
    vc = v - jnp.mean(v, axis=-1, keepdims=True)
    var = jnp.mean(vc * vc, axis=-1, keepdims=True)
    vn = (vc * lax.rsqrt(var + LN_EPS) * g_ref[...]).astype(jnp.bfloat16)
    row = lax.broadcasted_iota(jnp.int32, (chunk, chunk), 0)
    col = lax.broadcasted_iota(jnp.int32, (chunk, chunk), 1)
    for g in range(SGU_GROUPS):
        w = jnp.where(col <= row, w_ref[g], 0.0).astype(jnp.bfloat16)
        bias = bt_ref[:, g:g + 1]
        cols = slice(g * gd, (g + 1) * gd)
        for n in range(tt // chunk):
            rows = slice(n * chunk, (n + 1) * chunk)
            mixed = _dot(w, vn[rows, cols]) + bias
            u = _gelu_tanh(u_ref[0, rows, cols].astype(jnp.float32))
            y_ref[0, rows, cols] = (u * mixed).astype(y_ref.dtype)


def _sgu(proj, w_s, b_s, g_v, col0):
    b, t, _ = proj.shape
    width = g_v.shape[0]
    tt = 512
    c0 = col0 // width
    return pl.pallas_call(
        _sgu_kernel,
        grid=(b, t // tt),
        in_specs=[pl.BlockSpec((1, tt, width), lambda bi, ti: (bi, ti, c0)),
                  pl.BlockSpec((1, tt, width), lambda bi, ti: (bi, ti, c0 + 1)),
                  pl.BlockSpec(w_s.shape, lambda bi, ti: (0, 0, 0)),
                  pl.BlockSpec((SGU_CHUNK, SGU_GROUPS), lambda bi, ti: (0, 0)),
                  pl.BlockSpec((1, width), lambda bi, ti: (0, 0))],
        out_specs=pl.BlockSpec((1, tt, width), lambda bi, ti: (bi, ti, 0)),
        out_shape=jax.ShapeDtypeStruct((b, t, width), jnp.bfloat16),
        compiler_params=_cparams(2),
        name="sgu",
    )(proj, proj, w_s, b_s.T, g_v.reshape(1, width))


def kernel(x, c, w_ada, b_ada, ada_table, rel_bias, g_mix_pre, g_mix_post, g_ffn_pre, g_ffn_post,
           w_in, conv_qk, b_if, g_mlstm, w_sgu, b_sgu, g_sgu, w_out, w_gate, w_up, w_down):
    bsz, t, d = x.shape
    depth = w_in.shape[0]
    bf16 = jnp.bfloat16
    att_heads = rel_bias.shape[1]
    att_w = att_heads * ATT_HEAD_DIM
    ml_w = g_mlstm.shape[1]
    sgu_w = g_sgu.shape[1]
    n_gate = 2 * MLSTM_HEADS
    gate0 = 3 * att_w + 4 * ml_w
    ffn = w_gate.shape[2]
    m = bsz * t

    base_mod = _ada_base(c, w_ada, b_ada)
    diag_tiles, near_tiles = _rel_bias_tiles(rel_bias)

    for l in range(depth):
        tab = ada_table[l]
        w_if = jnp.pad(w_in[l, :, gate0:gate0 + n_gate], ((0, 0), (0, 128 - n_gate))).astype(bf16)
        w_uv = w_in[l, :, gate0 + n_gate:].astype(bf16)
        bias_row = jnp.pad(b_if[l].reshape(1, n_gate), ((0, 0), (0, 128 - n_gate)))

        h = _prenorm(x, g_mix_pre[l], base_mod, tab, 0, 1).reshape(m, d)
        proj = _matmul_wcast(h, w_in, l, gate0, 1024, 512, bf16, "in_proj").reshape(bsz, t, gate0)
        proj_uv = _matmul(h, w_uv, 1024, 1024, bf16, "in_proj_uv").reshape(bsz, t, 2 * sgu_w)
        if_pre = _matmul(h, w_if, 1024, 128, jnp.float32, "gate_proj").reshape(bsz, t, 128)

        y_att = _moba_attention(proj, rel_bias, diag_tiles, near_tiles, att_heads)
        g_rows, m_cols, e_cols = _mlstm_gates(if_pre, bias_row)
        y_ml = _mlstm(proj, conv_qk[l], g_rows, m_cols, e_cols, g_mlstm[l], 3 * att_w)
        y_sgu = _sgu(proj_uv, w_sgu[l], b_sgu[l], g_sgu[l], 0)
        y = _matmul3_wcast(y_att.reshape(m, att_w), y_ml.reshape(m, ml_w), y_sgu.reshape(m, sgu_w),
                           w_out, l, 1024, 512, jnp.float32, "out_proj")
        x = _postnorm_residual(x, y.reshape(bsz, t, d), g_mix_post[l], base_mod, tab, 2)

        h = _prenorm(x, g_ffn_pre[l], base_mod, tab, 3, 4).reshape(m, d)
        act = _swiglu_matmul(h, w_gate, w_up, l, 1024, 256)
        y = _matmul_kacc(act, w_down[l].astype(bf16), 1024, 512, ffn // 2, jnp.float32, "ffn_down")
        x = _postnorm_residual(x, y.reshape(bsz, t, d), g_ffn_post[l], base_mod, tab, 5)
    return x
```

```python
import functools
import math

import numpy as np
import jax
import jax.numpy as jnp
from jax import lax
from jax.experimental import pallas as pl
from jax.experimental.pallas import tpu as pltpu

ATT_HEAD_DIM = 128
MOBA_BLOCK = 256
MOBA_TOPK = 3
REL_BUCKETS = 32
REL_MAX_DIST = 128
MLSTM_HEADS = 4
MLSTM_BLOCK = 256
CONV_WIDTH = 4
SGU_GROUPS = 8
SGU_CHUNK = 128
N_MOD = 6
RMS_EPS = 1e-6
LN_EPS = 1e-5
MASKED = -1e30

VMEM_LIMIT_BYTES = 56 * 1024 * 1024


def _cparams(n_axes):
    return pltpu.CompilerParams(dimension_semantics=("arbitrary",) * n_axes,
                                vmem_limit_bytes=VMEM_LIMIT_BYTES)


def _dot(a, b):
    return jnp.dot(a, b, preferred_element_type=jnp.float32)


def _dot_nt(a, b):
    return lax.dot_general(a, b, (((1,), (1,)), ((), ())), preferred_element_type=jnp.float32)


def _dot_tn(a, b):
    return lax.dot_general(a, b, (((0,), (0,)), ((), ())), preferred_element_type=jnp.float32)


def _silu(x):
    return x * jax.nn.sigmoid(x)


def _gelu_tanh(x):
    return 0.5 * x * (1.0 + jnp.tanh(math.sqrt(2.0 / math.pi) * (x + 0.044715 * (x * x * x))))


def _ada_kernel(c_ref, w_ref, b_ref, o_ref):
    a = _silu(c_ref[...]).astype(jnp.bfloat16)
    o_ref[...] = _dot(a, w_ref[...].astype(jnp.bfloat16)) + b_ref[...]


def _ada_base(c, w_ada, b_ada):
    b, d = c.shape
    n = w_ada.shape[1]
    rows = 16
    c_pad = jnp.zeros((rows, d), jnp.float32).at[:b].set(c)
    tn = 1024
    out = pl.pallas_call(
        _ada_kernel,
        grid=(n // tn,),
        in_specs=[pl.BlockSpec((rows, d), lambda j: (0, 0)),
                  pl.BlockSpec((d, tn), lambda j: (0, j)),
                  pl.BlockSpec((1, tn), lambda j: (0, j))],
        out_specs=pl.BlockSpec((rows, tn), lambda j: (0, j)),
        out_shape=jax.ShapeDtypeStruct((rows, n), jnp.float32),
        compiler_params=_cparams(1),
        name="ada_base",
    )(c_pad, w_ada, b_ada.reshape(1, n))
    return out[:b].reshape(b, N_MOD, n // N_MOD)


def _prenorm_kernel(shift_idx, scale_idx, x_ref, g_ref, mod_ref, tab_ref, o_ref):
    x = x_ref[0]
    y = x * lax.rsqrt(jnp.mean(x * x, axis=-1, keepdims=True) + RMS_EPS) * g_ref[...]
    scale = mod_ref[0, scale_idx:scale_idx + 1, :] + tab_ref[scale_idx:scale_idx + 1, :]
    shift = mod_ref[0, shift_idx:shift_idx + 1, :] + tab_ref[shift_idx:shift_idx + 1, :]
    o_ref[0] = (y * (1.0 + scale) + shift).astype(o_ref.dtype)


def _prenorm(x, g, base_mod, tab, shift_idx, scale_idx):
    b, t, d = x.shape
    tt = 256
    return pl.pallas_call(
        functools.partial(_prenorm_kernel, shift_idx, scale_idx),
        grid=(b, t // tt),
        in_specs=[pl.BlockSpec((1, tt, d), lambda bi, ti: (bi, ti, 0)),
                  pl.BlockSpec((1, d), lambda bi, ti: (0, 0)),
                  pl.BlockSpec((1, N_MOD, d), lambda bi, ti: (bi, 0, 0)),
                  pl.BlockSpec((N_MOD, d), lambda bi, ti: (0, 0))],
        out_specs=pl.BlockSpec((1, tt, d), lambda bi, ti: (bi, ti, 0)),
        out_shape=jax.ShapeDtypeStruct((b, t, d), jnp.bfloat16),
        compiler_params=_cparams(2),
        name="prenorm",
    )(x, g.reshape(1, d), base_mod, tab)


def _postnorm_kernel(gate_idx, x_ref, y_ref, g_ref, mod_ref, tab_ref, o_ref):
    y = y_ref[0].astype(jnp.float32)
    yn = y * lax.rsqrt(jnp.mean(y * y, axis=-1, keepdims=True) + RMS_EPS) * g_ref[...]
    gate = mod_ref[0, gate_idx:gate_idx + 1, :] + tab_ref[gate_idx:gate_idx + 1, :]
    o_ref[0] = x_ref[0] + gate * yn


def _post_pre_kernel(gate_idx, shift_idx, scale_idx, x_ref, y_ref, g_ref, gn_ref, mod_ref, tab_ref, tabn_ref,
                     o_ref, h_ref):
    _postnorm_kernel(gate_idx, x_ref, y_ref, g_ref, mod_ref, tab_ref, o_ref)
    _prenorm_kernel(shift_idx, scale_idx, o_ref, gn_ref, mod_ref, tabn_ref, h_ref)


def _postnorm_prenorm(x, y, g_post, base_mod, tab, gate_idx, g_pre, tab_next, shift_idx, scale_idx):
    b, t, d = x.shape
    tt = 256
    row_spec = pl.BlockSpec((1, tt, d), lambda bi, ti: (bi, ti, 0))
    vec_spec = pl.BlockSpec((1, d), lambda bi, ti: (0, 0))
    tab_spec = pl.BlockSpec((N_MOD, d), lambda bi, ti: (0, 0))
    return pl.pallas_call(
        functools.partial(_post_pre_kernel, gate_idx, shift_idx, scale_idx),
        grid=(b, t // tt),
        in_specs=[row_spec, row_spec, vec_spec, vec_spec,
                  pl.BlockSpec((1, N_MOD, d), lambda bi, ti: (bi, 0, 0)), tab_spec, tab_spec],
        out_specs=[row_spec, row_spec],
        out_shape=[jax.ShapeDtypeStruct((b, t, d), jnp.float32), jax.ShapeDtypeStruct((b, t, d), jnp.bfloat16)],
        input_output_aliases={0: 0},
        compiler_params=_cparams(2),
        name="postnorm_prenorm",
    )(x, y, g_post.reshape(1, d), g_pre.reshape(1, d), base_mod, tab, tab_next)


def _postnorm_residual(x, y, g, base_mod, tab, gate_idx):
    b, t, d = x.shape
    tt = 256
    return pl.pallas_call(
        functools.partial(_postnorm_kernel, gate_idx),
        grid=(b, t // tt),
        in_specs=[pl.BlockSpec((1, tt, d), lambda bi, ti: (bi, ti, 0)),
                  pl.BlockSpec((1, tt, d), lambda bi, ti: (bi, ti, 0)),
                  pl.BlockSpec((1, d), lambda bi, ti: (0, 0)),
                  pl.BlockSpec((1, N_MOD, d), lambda bi, ti: (bi, 0, 0)),
                  pl.BlockSpec((N_MOD, d), lambda bi, ti: (0, 0))],
        out_specs=pl.BlockSpec((1, tt, d), lambda bi, ti: (bi, ti, 0)),
        out_shape=jax.ShapeDtypeStruct((b, t, d), jnp.float32),
        input_output_aliases={0: 0},
        compiler_params=_cparams(2),
        name="postnorm_residual",
    )(x, y, g.reshape(1, d), base_mod, tab)


def _mm_kernel(x_ref, w_ref, o_ref):
    o_ref[...] = _dot(x_ref[...], w_ref[...]).astype(o_ref.dtype)


def _matmul(x, w, tm, tn, out_dtype, name):
    m, k = x.shape
    n = w.shape[1]
    return pl.pallas_call(
        _mm_kernel,
        grid=(m // tm, n // tn),
        in_specs=[pl.BlockSpec((tm, k), lambda i, j: (i, 0)),
                  pl.BlockSpec((k, tn), lambda i, j: (0, j))],
        out_specs=pl.BlockSpec((tm, tn), lambda i, j: (i, j)),
        out_shape=jax.ShapeDtypeStruct((m, n), out_dtype),
        compiler_params=_cparams(2),
        name=name,
    )(x, w)


def _cast_weight_once(w_ref, wb_ref):
    @pl.when(pl.program_id(1) == 0)
    def _():
        wb_ref[...] = w_ref[0].astype(jnp.bfloat16)


def _mm_wcast_kernel(x_ref, w_ref, o_ref, wb_ref):
    _cast_weight_once(w_ref, wb_ref)
    o_ref[...] = _dot(x_ref[...], wb_ref[...]).astype(o_ref.dtype)


def _matmul_wcast(x, w_stack, layer, n_cols, tm, tn, out_dtype, name):
    m, k = x.shape
    return pl.pallas_call(
        _mm_wcast_kernel,
        grid=(n_cols // tn, m // tm),
        in_specs=[pl.BlockSpec((tm, k), lambda j, i: (i, 0)),
                  pl.BlockSpec((1, k, tn), lambda j, i: (layer, 0, j))],
        out_specs=pl.BlockSpec((tm, tn), lambda j, i: (i, j)),
        out_shape=jax.ShapeDtypeStruct((m, n_cols), out_dtype),
        scratch_shapes=[pltpu.VMEM((k, tn), jnp.bfloat16)],
        compiler_params=_cparams(2),
        name=name,
    )(x, w_stack)


def _mm_wcast_nt_kernel(x_ref, w_ref, o_ref, wb_ref):
    _cast_weight_once(w_ref, wb_ref)
    o_ref[...] = _dot_nt(x_ref[...], wb_ref[...]).astype(o_ref.dtype)


def _matmul_wcast_nt(x, wt_stack, layer, n_cols, tm, tn, out_dtype, name):
    m, k = x.shape
    return pl.pallas_call(
        _mm_wcast_nt_kernel,
        grid=(n_cols // tn, m // tm),
        in_specs=[pl.BlockSpec((tm, k), lambda j, i: (i, 0)),
                  pl.BlockSpec((1, tn, k), lambda j, i: (layer, j, 0))],
        out_specs=pl.BlockSpec((tm, tn), lambda j, i: (i, j)),
        out_shape=jax.ShapeDtypeStruct((m, n_cols), out_dtype),
        scratch_shapes=[pltpu.VMEM((tn, k), jnp.bfloat16)],
        compiler_params=_cparams(2),
        name=name,
    )(x, wt_stack)


def _mm3_wcast_kernel(xa_ref, xb_ref, xc_ref, w_ref, o_ref, wb_ref):
    _cast_weight_once(w_ref, wb_ref)
    ka, kb = xa_ref.shape[1], xb_ref.shape[1]
    o_ref[...] = (_dot(xa_ref[...], wb_ref[0:ka, :]) + _dot(xb_ref[...], wb_ref[ka:ka + kb, :])
                  + _dot(xc_ref[...], wb_ref[ka + kb:, :])).astype(o_ref.dtype)


def _matmul3_wcast(xa, xb, xc, w_stack, layer, tm, tn, out_dtype, name):
    m = xa.shape[0]
    k, n = w_stack.shape[1], w_stack.shape[2]
    x_spec = lambda x: pl.BlockSpec((tm, x.shape[1]), lambda j, i: (i, 0))
    return pl.pallas_call(
        _mm3_wcast_kernel,
        grid=(n // tn, m // tm),
        in_specs=[x_spec(xa), x_spec(xb), x_spec(xc),
                  pl.BlockSpec((1, k, tn), lambda j, i: (layer, 0, j))],
        out_specs=pl.BlockSpec((tm, tn), lambda j, i: (i, j)),
        out_shape=jax.ShapeDtypeStruct((m, n), out_dtype),
        scratch_shapes=[pltpu.VMEM((k, tn), jnp.bfloat16)],
        compiler_params=_cparams(2),
        name=name,
    )(xa, xb, xc, w_stack)


def _swiglu_wcast_kernel(x_ref, wg_ref, wu_ref, o_ref, wgb_ref, wub_ref):
    _cast_weight_once(wg_ref, wgb_ref)
    _cast_weight_once(wu_ref, wub_ref)
    x = x_ref[...]
    o_ref[...] = (_silu(_dot(x, wgb_ref[...])) * _dot(x, wub_ref[...])).astype(o_ref.dtype)


def _swiglu_matmul(x, wg_stack, wu_stack, layer, tm, tn):
    m, k = x.shape
    n = wg_stack.shape[2]
    w_spec = pl.BlockSpec((1, k, tn), lambda j, i: (layer, 0, j))
    return pl.pallas_call(
        _swiglu_wcast_kernel,
        grid=(n // tn, m // tm),
        in_specs=[pl.BlockSpec((tm, k), lambda j, i: (i, 0)), w_spec, w_spec],
        out_specs=pl.BlockSpec((tm, tn), lambda j, i: (i, j)),
        out_shape=jax.ShapeDtypeStruct((m, n), jnp.bfloat16),
        scratch_shapes=[pltpu.VMEM((k, tn), jnp.bfloat16), pltpu.VMEM((k, tn), jnp.bfloat16)],
        compiler_params=_cparams(2),
        name="ffn_gate_up",
    )(x, wg_stack, wu_stack)


def _mm_kacc_kernel(x_ref, w_ref, o_ref, acc_ref):
    kk = pl.program_id(2)

    @pl.when(kk == 0)
    def _():
        acc_ref[...] = jnp.zeros_like(acc_ref)

    acc_ref[...] += _dot(x_ref[...], w_ref[0])

    @pl.when(kk == pl.num_programs(2) - 1)
    def _():
        o_ref[...] = acc_ref[...].astype(o_ref.dtype)


def _matmul_kacc(x, w_stack, layer, tm, tn, tk, out_dtype, name):
    m, k = x.shape
    n = w_stack.shape[2]
    return pl.pallas_call(
        _mm_kacc_kernel,
        grid=(m // tm, n // tn, k // tk),
        in_specs=[pl.BlockSpec((tm, tk), lambda i, j, kk: (i, kk)),
                  pl.BlockSpec((1, tk, tn), lambda i, j, kk: (layer, kk, j))],
        out_specs=pl.BlockSpec((tm, tn), lambda i, j, kk: (i, j)),
        out_shape=jax.ShapeDtypeStruct((m, n), out_dtype),
        scratch_shapes=[pltpu.VMEM((tm, tn), jnp.float32)],
        compiler_params=_cparams(3),
        name=name,
    )(x, w_stack)


def _rel_bucket_starts():
    max_exact = REL_BUCKETS // 2
    d = np.arange(0, 4 * REL_MAX_DIST)
    ratio = np.maximum(d, 1).astype(np.float32) / np.float32(max_exact)
    large = max_exact + (np.log(ratio) / np.float32(math.log(REL_MAX_DIST / max_exact))
                         * np.float32(REL_BUCKETS - max_exact)).astype(np.int32)
    bucket = np.where(d < max_exact, d, np.minimum(large, REL_BUCKETS - 1))
    assert np.all(np.diff(bucket) >= 0)
    return [int(np.argmax(bucket == j)) for j in range(REL_BUCKETS)]


def _rel_tiles_kernel(starts, rel_ref, diag_ref, near_ref):
    h = pl.program_id(0)
    qry = lax.broadcasted_iota(jnp.int32, (MOBA_BLOCK, MOBA_BLOCK), 0)
    key = lax.broadcasted_iota(jnp.int32, (MOBA_BLOCK, MOBA_BLOCK), 1)
    dist = qry - key

    def bias_of(d):
        val = jnp.full(d.shape, rel_ref[0, h], jnp.float32)
        for j in range(1, REL_BUCKETS):
            val = jnp.where(d >= starts[j], rel_ref[j, h], val)
        return val

    diag_ref[0] = jnp.where(dist >= 0, bias_of(dist), MASKED)
    near_ref[0] = bias_of(dist + MOBA_BLOCK)


def _rel_bias_tiles(rel_bias):
    heads = rel_bias.shape[1]
    tile = jax.ShapeDtypeStruct((heads, MOBA_BLOCK, MOBA_BLOCK), jnp.float32)
    spec = pl.BlockSpec((1, MOBA_BLOCK, MOBA_BLOCK), lambda h: (h, 0, 0))
    return pl.pallas_call(
        functools.partial(_rel_tiles_kernel, _rel_bucket_starts()),
        grid=(heads,),
        in_specs=[pl.BlockSpec(memory_space=pltpu.SMEM)],
        out_specs=[spec, spec],
        out_shape=[tile, tile],
        compiler_params=_cparams(1),
        name="rel_bias_tiles",
    )(rel_bias)


def _moba_kernel(n_blocks, rel_ref, q_ref, k_ref, v_ref, diag_ref, near_ref, o_ref, kaug_ref):
    h = pl.program_id(1)
    blk, dh = MOBA_BLOCK, ATT_HEAD_DIM
    t = n_blocks * blk
    scale = dh ** -0.5
    far_bias = rel_ref[REL_BUCKETS - 1, h]
    k = k_ref[0]

    row = lax.broadcasted_iota(jnp.int32, (t, dh), 0)
    col = lax.broadcasted_iota(jnp.int32, (t, dh), 1)
    lo = col * blk
    kaug_ref[:, 0:dh] = k
    kaug_ref[:, dh:2 * dh] = jnp.where((row >= lo) & (row < lo + blk), 1.0, 0.0).astype(jnp.bfloat16)

    brow = lax.broadcasted_iota(jnp.int32, (n_blocks, t), 0) * blk
    bcol = lax.broadcasted_iota(jnp.int32, (n_blocks, t), 1)
    avg = jnp.where((bcol >= brow) & (bcol < brow + blk), 1.0 / blk, 0.0).astype(jnp.bfloat16)
    kmean = _dot(avg, k)
    km_hi = kmean.astype(jnp.bfloat16)
    km_lo = (kmean - km_hi.astype(jnp.float32)).astype(jnp.bfloat16)

    for i in range(n_blocks):
        rows = slice(i * blk, (i + 1) * blk)
        n_keys = (i + 1) * blk
        q = q_ref[0, rows, :]
        if i > MOBA_TOPK:
            s_gate = _dot_nt(km_hi, q) + _dot_nt(km_lo, q)
            bid = lax.broadcasted_iota(jnp.int32, s_gate.shape, 0)
            rank = jnp.zeros(s_gate.shape, jnp.int32)
            for j in range(i):
                s_j = s_gate[j:j + 1, :]
                beats = (s_j > s_gate) | ((s_j == s_gate) & (j < bid))
                rank = rank + jnp.where(beats, 1, 0)
            offs = jnp.where((bid < i) & (rank >= MOBA_TOPK), MASKED, 0.0)
            offs = jnp.concatenate([offs, jnp.zeros((dh - n_blocks, blk), jnp.float32)], axis=0)
            q_aug = jnp.concatenate([q, offs.T.astype(jnp.bfloat16)], axis=1)
            s = _dot_nt(q_aug, kaug_ref[0:n_keys, :])
        else:
            s = _dot_nt(q, k_ref[0, 0:n_keys, :])
        s = s * scale
        parts = []
        if i >= 2:
            parts.append(s[:, :(i - 1) * blk] + far_bias)
        if i >= 1:
            parts.append(s[:, (i - 1) * blk:i * blk] + near_ref[0])
        parts.append(s[:, i * blk:] + diag_ref[0])
        s = jnp.concatenate(parts, axis=1) if len(parts) > 1 else parts[0]
        p = jnp.exp(s - jnp.max(s, axis=1, keepdims=True))
        denom = jnp.sum(p, axis=1, keepdims=True)
        out = _dot(p.astype(jnp.bfloat16), v_ref[0, 0:n_keys, :]) / denom
        o_ref[0, rows, :] = out.astype(o_ref.dtype)


def _moba_attention(proj, rel_bias, diag_tiles, near_tiles, heads):
    b, t, _ = proj.shape
    blk, dh = MOBA_BLOCK, ATT_HEAD_DIM
    n_blocks = t // blk
    assert n_blocks <= dh
    tile_spec = pl.BlockSpec((1, blk, blk), lambda bi, h: (h, 0, 0))
    seq_spec = lambda off: pl.BlockSpec((1, t, dh), lambda bi, h: (bi, 0, off * heads + h))
    return pl.pallas_call(
        functools.partial(_moba_kernel, n_blocks),
        grid=(b, heads),
        in_specs=[pl.BlockSpec(memory_space=pltpu.SMEM), seq_spec(0), seq_spec(1), seq_spec(2),
                  tile_spec, tile_spec],
        out_specs=pl.BlockSpec((1, t, dh), lambda bi, h: (bi, 0, h)),
        out_shape=jax.ShapeDtypeStruct((b, t, heads * dh), jnp.bfloat16),
        scratch_shapes=[pltpu.VMEM((t, 2 * dh), jnp.bfloat16)],
        compiler_params=_cparams(2),
        name="moba_attention",
    )(rel_bias, proj, proj, proj, diag_tiles, near_tiles)


def _scan_rows(x, op, identity):
    n = x.shape[0]
    row = lax.broadcasted_iota(jnp.int32, x.shape, 0)
    s = 1
    while s < n:
        x = op(x, jnp.where(row >= s, pltpu.roll(x, s, axis=0), identity))
        s *= 2
    return x


def _mlstm_gates_kernel(if_ref, bias_ref, g_ref, mb_ref, eb_ref):
    pre = if_ref[0] + bias_ref[...]
    lanes = pre.shape[1]
    i_pre = pre
    f_pre = pltpu.roll(pre, lanes - MLSTM_HEADS, axis=1)
    log_f = -(jnp.maximum(-f_pre, 0.0) + jnp.log1p(jnp.exp(-jnp.abs(f_pre))))
    b = _scan_rows(log_f, jnp.add, 0.0)
    g = i_pre - b
    m_rel = jnp.maximum(_scan_rows(g, jnp.maximum, -jnp.inf), 0.0)
    e = jnp.exp(-(b + m_rel))
    g_t = g.T
    t = pre.shape[0]
    for hh in range(MLSTM_HEADS):
        g_ref[0, hh] = g_t[hh:hh + 1, :]
        mb_ref[0, hh] = jnp.broadcast_to(m_rel[:, hh:hh + 1], (t, lanes))
        eb_ref[0, hh] = jnp.broadcast_to(e[:, hh:hh + 1], (t, lanes))


def _mlstm_gates(if_pre, bias_row):
    b, t, lanes = if_pre.shape
    col = jax.ShapeDtypeStruct((b, MLSTM_HEADS, t, lanes), jnp.float32)
    col_spec = pl.BlockSpec((1, MLSTM_HEADS, t, lanes), lambda bi: (bi, 0, 0, 0))
    return pl.pallas_call(
        _mlstm_gates_kernel,
        grid=(b,),
        in_specs=[pl.BlockSpec((1, t, lanes), lambda bi: (bi, 0, 0)),
                  pl.BlockSpec((1, lanes), lambda bi: (0, 0))],
        out_specs=[pl.BlockSpec((1, MLSTM_HEADS, 1, t), lambda bi: (bi, 0, 0, 0)), col_spec, col_spec],
        out_shape=[jax.ShapeDtypeStruct((b, MLSTM_HEADS, 1, t), jnp.float32), col, col],
        compiler_params=_cparams(1),
        name="mlstm_gates",
    )(if_pre, bias_row)


def _causal_conv_silu(x, w):
    row = lax.broadcasted_iota(jnp.int32, x.shape, 0)
    out = x * w[CONV_WIDTH - 1:CONV_WIDTH, :]
    for back in range(1, CONV_WIDTH):
        shifted = jnp.where(row >= back, pltpu.roll(x, back, axis=0), 0.0)
        out = out + shifted * w[CONV_WIDTH - 1 - back:CONV_WIDTH - back, :]
    return _silu(out)


def _mlstm_kernel(q_ref, k_ref, v_ref, o_ref, wq_ref, wk_ref, g_ref, mb_ref, eb_ref, gn_ref, y_ref,
                  qc_ref, kc_ref):
    blk = MLSTM_BLOCK
    t, d = q_ref.shape[1], q_ref.shape[2]
    lanes = mb_ref.shape[3]
    qc_ref[...] = _causal_conv_silu(q_ref[0].astype(jnp.float32), wq_ref[...]).astype(jnp.bfloat16)
    kc = _causal_conv_silu(k_ref[0].astype(jnp.float32), wk_ref[...]) * (d ** -0.5)
    kc_ref[...] = kc.astype(jnp.bfloat16)
    qi = lax.broadcasted_iota(jnp.int32, (blk, blk), 0)
    ki = lax.broadcasted_iota(jnp.int32, (blk, blk), 1)

    for i in range(t // blk):
        rows = slice(i * blk, (i + 1) * blk)
        n_keys = (i + 1) * blk
        qk = _dot_nt(qc_ref[rows, :], kc_ref[0:n_keys, :])
        m_col = mb_ref[0, 0, rows, :]
        m_full = jnp.concatenate([m_col] * (n_keys // lanes), axis=1)
        w = jnp.exp(g_ref[0, 0, :, 0:n_keys] - m_full)
        w_own = jnp.where(ki <= qi, w[:, i * blk:], 0.0)
        w = jnp.concatenate([w[:, :i * blk], w_own], axis=1) if i else w_own
        s = qk * w
        den = jnp.sum(s, axis=-1, keepdims=True)
        num = _dot(s.astype(jnp.bfloat16), v_ref[0, 0:n_keys, :])
        hid = num / jnp.maximum(jnp.abs(den), eb_ref[0, 0, rows, 0:1])
        hn = hid * lax.rsqrt(jnp.mean(hid * hid, axis=-1, keepdims=True) + RMS_EPS) * gn_ref[...]
        y_ref[0, rows, :] = (jax.nn.sigmoid(o_ref[0, rows, :].astype(jnp.float32)) * hn).astype(y_ref.dtype)


def _mlstm(proj, conv_w, g_rows, m_cols, e_cols, g_norm, col0):
    b, t, _ = proj.shape
    heads = MLSTM_HEADS
    d = g_norm.shape[0] // heads
    c0 = col0 // d
    lanes = m_cols.shape[3]
    full = lambda off: pl.BlockSpec((1, t, d), lambda bi, h: (bi, 0, c0 + off * heads + h))
    col_spec = pl.BlockSpec((1, 1, t, lanes), lambda bi, h: (bi, h, 0, 0))
    return pl.pallas_call(
        _mlstm_kernel,
        grid=(b, heads),
        in_specs=[full(0), full(1), full(2), full(3),
                  pl.BlockSpec((CONV_WIDTH, d), lambda bi, h: (0, h)),
                  pl.BlockSpec((CONV_WIDTH, d), lambda bi, h: (0, heads + h)),
                  pl.BlockSpec((1, 1, 1, t), lambda bi, h: (bi, h, 0, 0)),
                  col_spec, col_spec,
                  pl.BlockSpec((1, d), lambda bi, h: (0, h))],
        out_specs=pl.BlockSpec((1, t, d), lambda bi, h: (bi, 0, h)),
        out_shape=jax.ShapeDtypeStruct((b, t, heads * d), jnp.bfloat16),
        scratch_shapes=[pltpu.VMEM((t, d), jnp.bfloat16), pltpu.VMEM((t, d), jnp.bfloat16)],
        compiler_params=_cparams(2),
        name="mlstm",
    )(proj, proj, proj, proj, conv_w, conv_w, g_rows, m_cols, e_cols, g_norm.reshape(1, -1))


def _sgu_kernel(u_ref, v_ref, w_ref, bt_ref, g_ref, y_ref):
    chunk = SGU_CHUNK
    tt, width = u_ref.shape[1], u_ref.shape[2]
    gd = width // SGU_GROUPS
    v = _gelu_tanh(v_ref[0].astype(jnp.float32))
    vc = v - jnp.mean(v, axis=-1, keepdims=True)
    var = jnp.mean(vc * vc, axis=-1, keepdims=True)
    vn = (vc * lax.rsqrt(var + LN_EPS) * g_ref[...]).astype(jnp.bfloat16)
    row = lax.broadcasted_iota(jnp.int32, (chunk, chunk), 0)
    col = lax.broadcasted_iota(jnp.int32, (chunk, chunk), 1)
    for g in range(SGU_GROUPS):
        w = jnp.where(col <= row, w_ref[g], 0.0).astype(jnp.bfloat16)
        bias = bt_ref[:, g:g + 1]
        cols = slice(g * gd, (g + 1) * gd)
        for n in range(tt // chunk):
            rows = slice(n * chunk, (n + 1) * chunk)
            mixed = _dot(w, vn[rows, cols]) + bias
            u = _gelu_tanh(u_ref[0, rows, cols].astype(jnp.float32))
            y_ref[0, rows, cols] = (u * mixed).astype(y_ref.dtype)


def _sgu(proj, w_s, b_s, g_v, col0):
    b, t, _ = proj.shape
    width = g_v.shape[0]
    tt = 512
    c0 = col0 // width
    return pl.pallas_call(
        _sgu_kernel,
        grid=(b, t // tt),
        in_specs=[pl.BlockSpec((1, tt, width), lambda bi, ti: (bi, ti, c0)),
                  pl.BlockSpec((1, tt, width), lambda bi, ti: (bi, ti, c0 + 1)),
                  pl.BlockSpec(w_s.shape, lambda bi, ti: (0, 0, 0)),
                  pl.BlockSpec((SGU_CHUNK, SGU_GROUPS), lambda bi, ti: (0, 0)),
                  pl.BlockSpec((1, width), lambda bi, ti: (0, 0))],
        out_specs=pl.BlockSpec((1, tt, width), lambda bi, ti: (bi, ti, 0)),
        out_shape=jax.ShapeDtypeStruct((b, t, width), jnp.bfloat16),
        compiler_params=_cparams(2),
        name="sgu",
    )(proj, proj, w_s, b_s.T, g_v.reshape(1, width))


def kernel(x, c, w_ada, b_ada, ada_table, rel_bias, g_mix_pre, g_mix_post, g_ffn_pre, g_ffn_post,
           w_in, conv_qk, b_if, g_mlstm, w_sgu, b_sgu, g_sgu, w_out, w_gate, w_up, w_down):
    bsz, t, d = x.shape
    depth = w_in.shape[0]
    bf16 = jnp.bfloat16
    att_heads = rel_bias.shape[1]
    att_w = att_heads * ATT_HEAD_DIM
    ml_w = g_mlstm.shape[1]
    sgu_w = g_sgu.shape[1]
    n_gate = 2 * MLSTM_HEADS
    gate0 = 3 * att_w + 4 * ml_w
    ffn = w_gate.shape[2]
    m = bsz * t

    base_mod = _ada_base(c, w_ada, b_ada)
    diag_tiles, near_tiles = _rel_bias_tiles(rel_bias)
    w_in_t = jnp.swapaxes(w_in, 1, 2)
    w_down_b = w_down.astype(bf16)

    h = _prenorm(x, g_mix_pre[0], base_mod, ada_table[0], 0, 1)
    for l in range(depth):
        tab = ada_table[l]
        w_if_t = jnp.pad(w_in_t[l:l + 1, gate0:gate0 + n_gate, :], ((0, 0), (0, 128 - n_gate), (0, 0)))
        w_uv_t = w_in_t[l:l + 1, gate0 + n_gate:, :]
        bias_row = jnp.pad(b_if[l].reshape(1, n_gate), ((0, 0), (0, 128 - n_gate)))

        h = h.reshape(m, d)
        proj = _matmul_wcast_nt(h, w_in_t, l, gate0, 1024, 512, bf16, "in_proj").reshape(bsz, t, gate0)
        proj_uv = _matmul_wcast_nt(h, w_uv_t, 0, 2 * sgu_w, 1024, 512, bf16, "in_proj_uv").reshape(bsz, t, 2 * sgu_w)
        if_pre = _matmul_wcast_nt(h, w_if_t, 0, 128, 1024, 128, jnp.float32, "gate_proj").reshape(bsz, t, 128)

        y_att = _moba_attention(proj, rel_bias, diag_tiles, near_tiles, att_heads)
        g_rows, m_cols, e_cols = _mlstm_gates(if_pre, bias_row)
        y_ml = _mlstm(proj, conv_qk[l], g_rows, m_cols, e_cols, g_mlstm[l], 3 * att_w)
        y_sgu = _sgu(proj_uv, w_sgu[l], b_sgu[l], g_sgu[l], 0)
        y = _matmul3_wcast(y_att.reshape(m, att_w), y_ml.reshape(m, ml_w), y_sgu.reshape(m, sgu_w),
                           w_out, l, 1024, 512, bf16, "out_proj")
        x, h = _postnorm_prenorm(x, y.reshape(bsz, t, d), g_mix_post[l], base_mod, tab, 2,
                                 g_ffn_pre[l], tab, 3, 4)

        act = _swiglu_matmul(h.reshape(m, d), w_gate, w_up, l, 1024, 256)
        y = _matmul_kacc(act, w_down_b, l, 1024, 512, ffn // 2, bf16, "ffn_down").reshape(bsz, t, d)
        if l + 1 < depth:
            x, h = _postnorm_prenorm(x, y, g_ffn_post[l], base_mod, tab, 5,
                                     g_mix_pre[l + 1], ada_table[l + 1], 0, 1)
        else:
            x = _postnorm_residual(x, y, g_ffn_post[l], base_mod, tab, 5)
    return x
```

```python
import functools
import math

import numpy as np
import jax
import jax.numpy as jnp
from jax import lax
from jax.experimental import pallas as pl
from jax.experimental.pallas import tpu as pltpu

ATT_HEAD_DIM = 128
MOBA_BLOCK = 256
MOBA_TOPK = 3
REL_BUCKETS = 32
REL_MAX_DIST = 128
MLSTM_HEADS = 4
MLSTM_BLOCK = 256
CONV_WIDTH = 4
SGU_GROUPS = 8
SGU_CHUNK = 128
N_MOD = 6
RMS_EPS = 1e-6
LN_EPS = 1e-5
MASKED = -1e30

VMEM_LIMIT_BYTES = 56 * 1024 * 1024


def _cparams(n_axes):
    return pltpu.CompilerParams(dimension_semantics=("arbitrary",) * n_axes,
                                vmem_limit_bytes=VMEM_LIMIT_BYTES)


def _dot(a, b):
    return jnp.dot(a, b, preferred_element_type=jnp.float32)


def _dot_nt(a, b):
    return lax.dot_general(a, b, (((1,), (1,)), ((), ())), preferred_element_type=jnp.float32)


def _dot_tn(a, b):
    return lax.dot_general(a, b, (((0,), (0,)), ((), ())), preferred_element_type=jnp.float32)


def _silu(x):
    return x * jax.nn.sigmoid(x)


def _gelu_tanh(x):
    return 0.5 * x * (1.0 + jnp.tanh(math.sqrt(2.0 / math.pi) * (x + 0.044715 * (x * x * x))))


def _ada_kernel(c_ref, w_ref, b_ref, o_ref):
    a = _silu(c_ref[...]).astype(jnp.bfloat16)
    o_ref[...] = _dot(a, w_ref[...].astype(jnp.bfloat16)) + b_ref[...]


def _ada_base(c, w_ada, b_ada):
    b, d = c.shape
    n = w_ada.shape[1]
    rows = 16
    c_pad = jnp.zeros((rows, d), jnp.float32).at[:b].set(c)
    tn = 1024
    out = pl.pallas_call(
        _ada_kernel,
        grid=(n // tn,),
        in_specs=[pl.BlockSpec((rows, d), lambda j: (0, 0)),
                  pl.BlockSpec((d, tn), lambda j: (0, j)),
                  pl.BlockSpec((1, tn), lambda j: (0, j))],
        out_specs=pl.BlockSpec((rows, tn), lambda j: (0, j)),
        out_shape=jax.ShapeDtypeStruct((rows, n), jnp.float32),
        compiler_params=_cparams(1),
        name="ada_base",
    )(c_pad, w_ada, b_ada.reshape(1, n))
    return out[:b].reshape(b, N_MOD, n // N_MOD)


def _prenorm_kernel(shift_idx, scale_idx, x_ref, g_ref, mod_ref, tab_ref, o_ref):
    x = x_ref[0]
    y = x * lax.rsqrt(jnp.mean(x * x, axis=-1, keepdims=True) + RMS_EPS) * g_ref[...]
    scale = mod_ref[0, scale_idx:scale_idx + 1, :] + tab_ref[scale_idx:scale_idx + 1, :]
    shift = mod_ref[0, shift_idx:shift_idx + 1, :] + tab_ref[shift_idx:shift_idx + 1, :]
    o_ref[0] = (y * (1.0 + scale) + shift).astype(o_ref.dtype)


def _prenorm(x, g, base_mod, tab, shift_idx, scale_idx):
    b, t, d = x.shape
    tt = 256
    return pl.pallas_call(
        functools.partial(_prenorm_kernel, shift_idx, scale_idx),
        grid=(b, t // tt),
        in_specs=[pl.BlockSpec((1, tt, d), lambda bi, ti: (bi, ti, 0)),
                  pl.BlockSpec((1, d), lambda bi, ti: (0, 0)),
                  pl.BlockSpec((1, N_MOD, d), lambda bi, ti: (bi, 0, 0)),
                  pl.BlockSpec((N_MOD, d), lambda bi, ti: (0, 0))],
        out_specs=pl.BlockSpec((1, tt, d), lambda bi, ti: (bi, ti, 0)),
        out_shape=jax.ShapeDtypeStruct((b, t, d), jnp.bfloat16),
        compiler_params=_cparams(2),
        name="prenorm",
    )(x, g.reshape(1, d), base_mod, tab)


def _postnorm_kernel(gate_idx, x_ref, y_ref, g_ref, mod_ref, tab_ref, o_ref):
    y = y_ref[0].astype(jnp.float32)
    yn = y * lax.rsqrt(jnp.mean(y * y, axis=-1, keepdims=True) + RMS_EPS) * g_ref[...]
    gate = mod_ref[0, gate_idx:gate_idx + 1, :] + tab_ref[gate_idx:gate_idx + 1, :]
    o_ref[0] = x_ref[0] + gate * yn


def _post_pre_kernel(gate_idx, shift_idx, scale_idx, x_ref, y_ref, g_ref, gn_ref, mod_ref, tab_ref, tabn_ref,
                     o_ref, h_ref):
    _postnorm_kernel(gate_idx, x_ref, y_ref, g_ref, mod_ref, tab_ref, o_ref)
    _prenorm_kernel(shift_idx, scale_idx, o_ref, gn_ref, mod_ref, tabn_ref, h_ref)


def _postnorm_prenorm(x, y, g_post, base_mod, tab, gate_idx, g_pre, tab_next, shift_idx, scale_idx):
    b, t, d = x.shape
    tt = 256
    row_spec = pl.BlockSpec((1, tt, d), lambda bi, ti: (bi, ti, 0))
    vec_spec = pl.BlockSpec((1, d), lambda bi, ti: (0, 0))
    tab_spec = pl.BlockSpec((N_MOD, d), lambda bi, ti: (0, 0))
    return pl.pallas_call(
        functools.partial(_post_pre_kernel, gate_idx, shift_idx, scale_idx),
        grid=(b, t // tt),
        in_specs=[row_spec, row_spec, vec_spec, vec_spec,
                  pl.BlockSpec((1, N_MOD, d), lambda bi, ti: (bi, 0, 0)), tab_spec, tab_spec],
        out_specs=[row_spec, row_spec],
        out_shape=[jax.ShapeDtypeStruct((b, t, d), jnp.float32), jax.ShapeDtypeStruct((b, t, d), jnp.bfloat16)],
        input_output_aliases={0: 0},
        compiler_params=_cparams(2),
        name="postnorm_prenorm",
    )(x, y, g_post.reshape(1, d), g_pre.reshape(1, d), base_mod, tab, tab_next)


def _postnorm_residual(x, y, g, base_mod, tab, gate_idx):
    b, t, d = x.shape
    tt = 256
    return pl.pallas_call(
        functools.partial(_postnorm_kernel, gate_idx),
        grid=(b, t // tt),
        in_specs=[pl.BlockSpec((1, tt, d), lambda bi, ti: (bi, ti, 0)),
                  pl.BlockSpec((1, tt, d), lambda bi, ti: (bi, ti, 0)),
                  pl.BlockSpec((1, d), lambda bi, ti: (0, 0)),
                  pl.BlockSpec((1, N_MOD, d), lambda bi, ti: (bi, 0, 0)),
                  pl.BlockSpec((N_MOD, d), lambda bi, ti: (0, 0))],
        out_specs=pl.BlockSpec((1, tt, d), lambda bi, ti: (bi, ti, 0)),
        out_shape=jax.ShapeDtypeStruct((b, t, d), jnp.float32),
        input_output_aliases={0: 0},
        compiler_params=_cparams(2),
        name="postnorm_residual",
    )(x, y, g.reshape(1, d), base_mod, tab)


def _stage_weight_piece(w_ref, wb_ref):
    t, i = pl.program_id(0), pl.program_id(1)
    rows = w_ref.shape[1]
    wb_ref[t % 2, pl.ds(pl.multiple_of(i * rows, rows), rows), :] = w_ref[0].astype(jnp.bfloat16)


def _staged_weight(wb_ref):
    return wb_ref[(pl.program_id(0) + 1) % 2]


def _stream_specs(tm, n_tiles, x_widths, out_tn):
    row = lambda t, i: jnp.where(t > 0, i, 0)
    x_specs = [pl.BlockSpec((tm, kx), lambda t, i: (row(t, i), 0)) for kx in x_widths]
    out_spec = pl.BlockSpec((tm, out_tn), lambda t, i: (row(t, i), jnp.maximum(t - 1, 0)))
    return x_specs, out_spec


def _mm_stream_nt_kernel(x_ref, w_ref, o_ref, wb_ref):
    _stage_weight_piece(w_ref, wb_ref)

    @pl.when(pl.program_id(0) > 0)
    def _():
        o_ref[...] = _dot_nt(x_ref[...], _staged_weight(wb_ref)).astype(o_ref.dtype)


def _matmul_stream_nt(x, wt_stack, layer, n_cols, tm, tn, out_dtype, name):
    m, k = x.shape
    n_tiles, n_rows = n_cols // tn, m // tm
    piece = tn // n_rows
    x_specs, out_spec = _stream_specs(tm, n_tiles, [k], tn)
    return pl.pallas_call(
        _mm_stream_nt_kernel,
        grid=(n_tiles + 1, n_rows),
        in_specs=x_specs + [pl.BlockSpec(
            (1, piece, k), lambda t, i: (layer, jnp.minimum(t, n_tiles - 1) * n_rows + i, 0))],
        out_specs=out_spec,
        out_shape=jax.ShapeDtypeStruct((m, n_cols), out_dtype),
        scratch_shapes=[pltpu.VMEM((2, tn, k), jnp.bfloat16)],
        compiler_params=_cparams(2),
        name=name,
    )(x, wt_stack)


def _cast_weight_once(w_ref, wb_ref):
    @pl.when(pl.program_id(1) == 0)
    def _():
        wb_ref[...] = w_ref[0].astype(jnp.bfloat16)


def _mm_wcast_nt_kernel(x_ref, w_ref, o_ref, wb_ref):
    _cast_weight_once(w_ref, wb_ref)
    o_ref[...] = _dot_nt(x_ref[...], wb_ref[...]).astype(o_ref.dtype)


def _matmul_wcast_nt(x, wt_stack, layer, n_cols, tm, tn, out_dtype, name):
    m, k = x.shape
    return pl.pallas_call(
        _mm_wcast_nt_kernel,
        grid=(n_cols // tn, m // tm),
        in_specs=[pl.BlockSpec((tm, k), lambda j, i: (i, 0)),
                  pl.BlockSpec((1, tn, k), lambda j, i: (layer, j, 0))],
        out_specs=pl.BlockSpec((tm, tn), lambda j, i: (i, j)),
        out_shape=jax.ShapeDtypeStruct((m, n_cols), out_dtype),
        scratch_shapes=[pltpu.VMEM((tn, k), jnp.bfloat16)],
        compiler_params=_cparams(2),
        name=name,
    )(x, wt_stack)


def _stream_weight_spec(w_stack, layer, tn, n_rows):
    k, n = w_stack.shape[1], w_stack.shape[2]
    n_tiles = n // tn
    return pl.BlockSpec((1, k // n_rows, tn), lambda t, i: (layer, i, jnp.minimum(t, n_tiles - 1)))


def _mm3_stream_kernel(xa_ref, xb_ref, xc_ref, w_ref, o_ref, wb_ref):
    _stage_weight_piece(w_ref, wb_ref)

    @pl.when(pl.program_id(0) > 0)
    def _():
        ka, kb = xa_ref.shape[1], xb_ref.shape[1]
        w = _staged_weight(wb_ref)
        o_ref[...] = (_dot(xa_ref[...], w[0:ka, :]) + _dot(xb_ref[...], w[ka:ka + kb, :])
                      + _dot(xc_ref[...], w[ka + kb:, :])).astype(o_ref.dtype)


def _matmul3_stream(xa, xb, xc, w_stack, layer, tm, tn, out_dtype, name):
    m = xa.shape[0]
    k, n = w_stack.shape[1], w_stack.shape[2]
    n_tiles, n_rows = n // tn, m // tm
    x_specs, out_spec = _stream_specs(tm, n_tiles, [xa.shape[1], xb.shape[1], xc.shape[1]], tn)
    return pl.pallas_call(
        _mm3_stream_kernel,
        grid=(n_tiles + 1, n_rows),
        in_specs=x_specs + [_stream_weight_spec(w_stack, layer, tn, n_rows)],
        out_specs=out_spec,
        out_shape=jax.ShapeDtypeStruct((m, n), out_dtype),
        scratch_shapes=[pltpu.VMEM((2, k, tn), jnp.bfloat16)],
        compiler_params=_cparams(2),
        name=name,
    )(xa, xb, xc, w_stack)


def _swiglu_stream_kernel(x_ref, wg_ref, wu_ref, o_ref, wgb_ref, wub_ref):
    _stage_weight_piece(wg_ref, wgb_ref)
    _stage_weight_piece(wu_ref, wub_ref)

    @pl.when(pl.program_id(0) > 0)
    def _():
        x = x_ref[...]
        gate = _dot(x, _staged_weight(wgb_ref))
        o_ref[...] = (_silu(gate) * _dot(x, _staged_weight(wub_ref))).astype(o_ref.dtype)


def _swiglu_matmul(x, wg_stack, wu_stack, layer, tm, tn):
    m, k = x.shape
    n = wg_stack.shape[2]
    n_tiles, n_rows = n // tn, m // tm
    x_specs, out_spec = _stream_specs(tm, n_tiles, [k], tn)
    w_spec = _stream_weight_spec(wg_stack, layer, tn, n_rows)
    return pl.pallas_call(
        _swiglu_stream_kernel,
        grid=(n_tiles + 1, n_rows),
        in_specs=x_specs + [w_spec, w_spec],
        out_specs=out_spec,
        out_shape=jax.ShapeDtypeStruct((m, n), jnp.bfloat16),
        scratch_shapes=[pltpu.VMEM((2, k, tn), jnp.bfloat16), pltpu.VMEM((2, k, tn), jnp.bfloat16)],
        compiler_params=_cparams(2),
        name="ffn_gate_up",
    )(x, wg_stack, wu_stack)


def _mm_kacc_kernel(x_ref, w_ref, o_ref, acc_ref):
    kk = pl.program_id(2)

    @pl.when(kk == 0)
    def _():
        acc_ref[...] = jnp.zeros_like(acc_ref)

    acc_ref[...] += _dot(x_ref[...], w_ref[0])

    @pl.when(kk == pl.num_programs(2) - 1)
    def _():
        o_ref[...] = acc_ref[...].astype(o_ref.dtype)


def _matmul_kacc(x, w_stack, layer, tm, tn, tk, out_dtype, name):
    m, k = x.shape
    n = w_stack.shape[2]
    return pl.pallas_call(
        _mm_kacc_kernel,
        grid=(m // tm, n // tn, k // tk),
        in_specs=[pl.BlockSpec((tm, tk), lambda i, j, kk: (i, kk)),
                  pl.BlockSpec((1, tk, tn), lambda i, j, kk: (layer, kk, j))],
        out_specs=pl.BlockSpec((tm, tn), lambda i, j, kk: (i, j)),
        out_shape=jax.ShapeDtypeStruct((m, n), out_dtype),
        scratch_shapes=[pltpu.VMEM((tm, tn), jnp.float32)],
        compiler_params=_cparams(3),
        name=name,
    )(x, w_stack)


def _rel_bucket_starts():
    max_exact = REL_BUCKETS // 2
    d = np.arange(0, 4 * REL_MAX_DIST)
    ratio = np.maximum(d, 1).astype(np.float32) / np.float32(max_exact)
    large = max_exact + (np.log(ratio) / np.float32(math.log(REL_MAX_DIST / max_exact))
                         * np.float32(REL_BUCKETS - max_exact)).astype(np.int32)
    bucket = np.where(d < max_exact, d, np.minimum(large, REL_BUCKETS - 1))
    assert np.all(np.diff(bucket) >= 0)
    return [int(np.argmax(bucket == j)) for j in range(REL_BUCKETS)]


def _rel_tiles_kernel(starts, rel_ref, diag_ref, near_ref):
    h = pl.program_id(0)
    qry = lax.broadcasted_iota(jnp.int32, (MOBA_BLOCK, MOBA_BLOCK), 0)
    key = lax.broadcasted_iota(jnp.int32, (MOBA_BLOCK, MOBA_BLOCK), 1)
    dist = qry - key

    def bias_of(d):
        val = jnp.full(d.shape, rel_ref[0, h], jnp.float32)
        for j in range(1, REL_BUCKETS):
            val = jnp.where(d >= starts[j], rel_ref[j, h], val)
        return val

    diag_ref[0] = jnp.where(dist >= 0, bias_of(dist), MASKED)
    near_ref[0] = bias_of(dist + MOBA_BLOCK)


def _rel_bias_tiles(rel_bias):
    heads = rel_bias.shape[1]
    tile = jax.ShapeDtypeStruct((heads, MOBA_BLOCK, MOBA_BLOCK), jnp.float32)
    spec = pl.BlockSpec((1, MOBA_BLOCK, MOBA_BLOCK), lambda h: (h, 0, 0))
    return pl.pallas_call(
        functools.partial(_rel_tiles_kernel, _rel_bucket_starts()),
        grid=(heads,),
        in_specs=[pl.BlockSpec(memory_space=pltpu.SMEM)],
        out_specs=[spec, spec],
        out_shape=[tile, tile],
        compiler_params=_cparams(1),
        name="rel_bias_tiles",
    )(rel_bias)


def _moba_kernel(n_blocks, rel_ref, q_ref, k_ref, v_ref, diag_ref, near_ref, o_ref,
                 kaug_ref, qaug_ref, s_ref, p_ref, l_ref, acc_ref):
    h = pl.program_id(1)
    blk, dh = MOBA_BLOCK, ATT_HEAD_DIM
    t = n_blocks * blk
    scale = dh ** -0.5
    far_bias = rel_ref[REL_BUCKETS - 1, h]
    k = k_ref[0]

    row = lax.broadcasted_iota(jnp.int32, (t, dh), 0)
    col = lax.broadcasted_iota(jnp.int32, (t, dh), 1)
    lo = col * blk
    kaug_ref[:, 0:dh] = k
    kaug_ref[:, dh:2 * dh] = jnp.where((row >= lo) & (row < lo + blk), 1.0, 0.0).astype(jnp.bfloat16)

    brow = lax.broadcasted_iota(jnp.int32, (n_blocks, t), 0) * blk
    bcol = lax.broadcasted_iota(jnp.int32, (n_blocks, t), 1)
    avg = jnp.where((bcol >= brow) & (bcol < brow + blk), 1.0 / blk, 0.0).astype(jnp.bfloat16)
    kmean = _dot(avg, k)
    km_hi = kmean.astype(jnp.bfloat16)
    km_lo = (kmean - km_hi.astype(jnp.float32)).astype(jnp.bfloat16)

    q = q_ref[0]
    s_gate = _dot_nt(km_hi, q) + _dot_nt(km_lo, q)
    bid = lax.broadcasted_iota(jnp.int32, s_gate.shape, 0)
    past = (bid + 1) * blk <= lax.broadcasted_iota(jnp.int32, s_gate.shape, 1)
    s_gate = jnp.where(past, s_gate, -jnp.inf)
    rank = jnp.zeros(s_gate.shape, jnp.int32)
    for j in range(n_blocks):
        s_j = s_gate[j:j + 1, :]
        beats = (s_j > s_gate) | ((s_j == s_gate) & (j < bid))
        rank = rank + jnp.where(beats, 1, 0)
    offs = jnp.where(past & (rank >= MOBA_TOPK), MASKED, 0.0)
    offs = jnp.concatenate([offs, jnp.zeros((dh - n_blocks, t), jnp.float32)], axis=0)
    qaug_ref[:, 0:dh] = q
    qaug_ref[:, dh:2 * dh] = offs.T.astype(jnp.bfloat16)

    for j in range(n_blocks):
        cols = slice(j * blk, (j + 1) * blk)
        s_ref[j * blk:, cols] = _dot_nt(qaug_ref[j * blk:, :], kaug_ref[cols, :])

    for i in range(n_blocks):
        rows = slice(i * blk, (i + 1) * blk)
        n_keys = (i + 1) * blk
        s = s_ref[rows, 0:n_keys] * scale
        parts = []
        if i >= 2:
            parts.append(s[:, :(i - 1) * blk] + far_bias)
        if i >= 1:
            parts.append(s[:, (i - 1) * blk:i * blk] + near_ref[0])
        parts.append(s[:, i * blk:] + diag_ref[0])
        s = jnp.concatenate(parts, axis=1) if len(parts) > 1 else parts[0]
        p = jnp.exp(s - jnp.max(s, axis=1, keepdims=True))
        l_ref[rows, :] = jnp.sum(p, axis=1, keepdims=True)
        p_ref[rows, 0:n_keys] = p.astype(jnp.bfloat16)

    for j in range(n_blocks):
        cols = slice(j * blk, (j + 1) * blk)
        pv = _dot(p_ref[j * blk:, cols], v_ref[0, cols, :])
        if j == 0:
            acc_ref[...] = pv
        else:
            acc_ref[j * blk:, :] += pv
    o_ref[0] = (acc_ref[...] / l_ref[...]).astype(o_ref.dtype)


def _moba_attention(proj, rel_bias, diag_tiles, near_tiles, heads):
    b, t, _ = proj.shape
    blk, dh = MOBA_BLOCK, ATT_HEAD_DIM
    n_blocks = t // blk
    assert n_blocks <= dh
    tile_spec = pl.BlockSpec((1, blk, blk), lambda bi, h: (h, 0, 0))
    seq_spec = lambda off: pl.BlockSpec((1, t, dh), lambda bi, h: (bi, 0, off * heads + h))
    return pl.pallas_call(
        functools.partial(_moba_kernel, n_blocks),
        grid=(b, heads),
        in_specs=[pl.BlockSpec(memory_space=pltpu.SMEM), seq_spec(0), seq_spec(1), seq_spec(2),
                  tile_spec, tile_spec],
        out_specs=pl.BlockSpec((1, t, dh), lambda bi, h: (bi, 0, h)),
        out_shape=jax.ShapeDtypeStruct((b, t, heads * dh), jnp.bfloat16),
        scratch_shapes=[pltpu.VMEM((t, 2 * dh), jnp.bfloat16),
                        pltpu.VMEM((t, 2 * dh), jnp.bfloat16),
                        pltpu.VMEM((t, t), jnp.float32),
                        pltpu.VMEM((t, t), jnp.bfloat16),
                        pltpu.VMEM((t, 1), jnp.float32),
                        pltpu.VMEM((t, dh), jnp.float32)],
        compiler_params=_cparams(2),
        name="moba_attention",
    )(rel_bias, proj, proj, proj, diag_tiles, near_tiles)


def _scan_rows(x, op, identity):
    n = x.shape[0]
    row = lax.broadcasted_iota(jnp.int32, x.shape, 0)
    s = 1
    while s < n:
        x = op(x, jnp.where(row >= s, pltpu.roll(x, s, axis=0), identity))
        s *= 2
    return x


def _mlstm_gates_kernel(if_ref, bias_ref, g_ref, mb_ref, eb_ref):
    pre = if_ref[0] + bias_ref[...]
    lanes = pre.shape[1]
    i_pre = pre
    f_pre = pltpu.roll(pre, lanes - MLSTM_HEADS, axis=1)
    log_f = -(jnp.maximum(-f_pre, 0.0) + jnp.log1p(jnp.exp(-jnp.abs(f_pre))))
    b = _scan_rows(log_f, jnp.add, 0.0)
    g = i_pre - b
    m_rel = jnp.maximum(_scan_rows(g, jnp.maximum, -jnp.inf), 0.0)
    e = jnp.exp(-(b + m_rel))
    g_t = g.T
    t = pre.shape[0]
    for hh in range(MLSTM_HEADS):
        g_ref[0, hh] = g_t[hh:hh + 1, :]
        mb_ref[0, hh] = jnp.broadcast_to(m_rel[:, hh:hh + 1], (t, lanes))
        eb_ref[0, hh] = jnp.broadcast_to(e[:, hh:hh + 1], (t, lanes))


def _mlstm_gates(if_pre, bias_row):
    b, t, lanes = if_pre.shape
    col = jax.ShapeDtypeStruct((b, MLSTM_HEADS, t, lanes), jnp.float32)
    col_spec = pl.BlockSpec((1, MLSTM_HEADS, t, lanes), lambda bi: (bi, 0, 0, 0))
    return pl.pallas_call(
        _mlstm_gates_kernel,
        grid=(b,),
        in_specs=[pl.BlockSpec((1, t, lanes), lambda bi: (bi, 0, 0)),
                  pl.BlockSpec((1, lanes), lambda bi: (0, 0))],
        out_specs=[pl.BlockSpec((1, MLSTM_HEADS, 1, t), lambda bi: (bi, 0, 0, 0)), col_spec, col_spec],
        out_shape=[jax.ShapeDtypeStruct((b, MLSTM_HEADS, 1, t), jnp.float32), col, col],
        compiler_params=_cparams(1),
        name="mlstm_gates",
    )(if_pre, bias_row)


def _causal_conv_silu(x, w):
    row = lax.broadcasted_iota(jnp.int32, x.shape, 0)
    out = x * w[CONV_WIDTH - 1:CONV_WIDTH, :]
    for back in range(1, CONV_WIDTH):
        shifted = jnp.where(row >= back, pltpu.roll(x, back, axis=0), 0.0)
        out = out + shifted * w[CONV_WIDTH - 1 - back:CONV_WIDTH - back, :]
    return _silu(out)


def _mlstm_kernel(q_ref, k_ref, v_ref, o_ref, wq_ref, wk_ref, g_ref, mb_ref, eb_ref, gn_ref, y_ref,
                  qc_ref, kc_ref):
    blk = MLSTM_BLOCK
    t, d = q_ref.shape[1], q_ref.shape[2]
    lanes = mb_ref.shape[3]
    qc_ref[...] = _causal_conv_silu(q_ref[0].astype(jnp.float32), wq_ref[...]).astype(jnp.bfloat16)
    kc = _causal_conv_silu(k_ref[0].astype(jnp.float32), wk_ref[...]) * (d ** -0.5)
    kc_ref[...] = kc.astype(jnp.bfloat16)
    qi = lax.broadcasted_iota(jnp.int32, (blk, blk), 0)
    ki = lax.broadcasted_iota(jnp.int32, (blk, blk), 1)

    for i in range(t // blk):
        rows = slice(i * blk, (i + 1) * blk)
        n_keys = (i + 1) * blk
        qk = _dot_nt(qc_ref[rows, :], kc_ref[0:n_keys, :])
        m_col = mb_ref[0, 0, rows, :]
        m_full = jnp.concatenate([m_col] * (n_keys // lanes), axis=1)
        w = jnp.exp(g_ref[0, 0, :, 0:n_keys] - m_full)
        w_own = jnp.where(ki <= qi, w[:, i * blk:], 0.0)
        w = jnp.concatenate([w[:, :i * blk], w_own], axis=1) if i else w_own
        s = qk * w
        den = jnp.sum(s, axis=-1, keepdims=True)
        num = _dot(s.astype(jnp.bfloat16), v_ref[0, 0:n_keys, :])
        hid = num / jnp.maximum(jnp.abs(den), eb_ref[0, 0, rows, 0:1])
        hn = hid * lax.rsqrt(jnp.mean(hid * hid, axis=-1, keepdims=True) + RMS_EPS) * gn_ref[...]
        y_ref[0, rows, :] = (jax.nn.sigmoid(o_ref[0, rows, :].astype(jnp.float32)) * hn).astype(y_ref.dtype)


def _mlstm(proj, conv_w, g_rows, m_cols, e_cols, g_norm, col0):
    b, t, _ = proj.shape
    heads = MLSTM_HEADS
    d = g_norm.shape[0] // heads
    c0 = col0 // d
    lanes = m_cols.shape[3]
    full = lambda off: pl.BlockSpec((1, t, d), lambda bi, h: (bi, 0, c0 + off * heads + h))
    col_spec = pl.BlockSpec((1, 1, t, lanes), lambda bi, h: (bi, h, 0, 0))
    return pl.pallas_call(
        _mlstm_kernel,
        grid=(b, heads),
        in_specs=[full(0), full(1), full(2), full(3),
                  pl.BlockSpec((CONV_WIDTH, d), lambda bi, h: (0, h)),
                  pl.BlockSpec((CONV_WIDTH, d), lambda bi, h: (0, heads + h)),
                  pl.BlockSpec((1, 1, 1, t), lambda bi, h: (bi, h, 0, 0)),
                  col_spec, col_spec,
                  pl.BlockSpec((1, d), lambda bi, h: (0, h))],
        out_specs=pl.BlockSpec((1, t, d), lambda bi, h: (bi, 0, h)),
        out_shape=jax.ShapeDtypeStruct((b, t, heads * d), jnp.bfloat16),
        scratch_shapes=[pltpu.VMEM((t, d), jnp.bfloat16), pltpu.VMEM((t, d), jnp.bfloat16)],
        compiler_params=_cparams(2),
        name="mlstm",
    )(proj, proj, proj, proj, conv_w, conv_w, g_rows, m_cols, e_cols, g_norm.reshape(1, -1))


def _sgu_kernel(u_ref, v_ref, w_ref, bt_ref, g_ref, y_ref):
    chunk = SGU_CHUNK
    tt, width = u_ref.shape[1], u_ref.shape[2]
    gd = width // SGU_GROUPS
    v = _gelu_tanh(v_ref[0].astype(jnp.float32))
    vc = v - jnp.mean(v, axis=-1, keepdims=True)
    var = jnp.mean(vc * vc, axis=-1, keepdims=True)
    vn = (vc * lax.rsqrt(var + LN_EPS) * g_ref[...]).astype(jnp.bfloat16)
    row = lax.broadcasted_iota(jnp.int32, (chunk, chunk), 0)
    col = lax.broadcasted_iota(jnp.int32, (chunk, chunk), 1)
    for g in range(SGU_GROUPS):
        w = jnp.where(col <= row, w_ref[g], 0.0).astype(jnp.bfloat16)
        bias = bt_ref[:, g:g + 1]
        cols = slice(g * gd, (g + 1) * gd)
        for n in range(tt // chunk):
            rows = slice(n * chunk, (n + 1) * chunk)
            mixed = _dot(w, vn[rows, cols]) + bias
            u = _gelu_tanh(u_ref[0, rows, cols].astype(jnp.float32))
            y_ref[0, rows, cols] = (u * mixed).astype(y_ref.dtype)


def _sgu(proj, w_s, b_s, g_v, col0):
    b, t, _ = proj.shape
    width = g_v.shape[0]
    tt = 512
    c0 = col0 // width
    return pl.pallas_call(
        _sgu_kernel,
        grid=(b, t // tt),
        in_specs=[pl.BlockSpec((1, tt, width), lambda bi, ti: (bi, ti, c0)),
                  pl.BlockSpec((1, tt, width), lambda bi, ti: (bi, ti, c0 + 1)),
                  pl.BlockSpec(w_s.shape, lambda bi, ti: (0, 0, 0)),
                  pl.BlockSpec((SGU_CHUNK, SGU_GROUPS), lambda bi, ti: (0, 0)),
                  pl.BlockSpec((1, width), lambda bi, ti: (0, 0))],
        out_specs=pl.BlockSpec((1, tt, width), lambda bi, ti: (bi, ti, 0)),
        out_shape=jax.ShapeDtypeStruct((b, t, width), jnp.bfloat16),
        compiler_params=_cparams(2),
        name="sgu",
    )(proj, proj, w_s, b_s.T, g_v.reshape(1, width))


def kernel(x, c, w_ada, b_ada, ada_table, rel_bias, g_mix_pre, g_mix_post, g_ffn_pre, g_ffn_post,
           w_in, conv_qk, b_if, g_mlstm, w_sgu, b_sgu, g_sgu, w_out, w_gate, w_up, w_down):
    bsz, t, d = x.shape
    depth = w_in.shape[0]
    bf16 = jnp.bfloat16
    att_heads = rel_bias.shape[1]
    att_w = att_heads * ATT_HEAD_DIM
    ml_w = g_mlstm.shape[1]
    sgu_w = g_sgu.shape[1]
    n_gate = 2 * MLSTM_HEADS
    gate0 = 3 * att_w + 4 * ml_w
    ffn = w_gate.shape[2]
    m = bsz * t

    base_mod = _ada_base(c, w_ada, b_ada)
    diag_tiles, near_tiles = _rel_bias_tiles(rel_bias)
    w_in_t = jnp.swapaxes(w_in, 1, 2)
    w_down_b = w_down.astype(bf16)

    h = _prenorm(x, g_mix_pre[0], base_mod, ada_table[0], 0, 1)
    for l in range(depth):
        tab = ada_table[l]
        w_if_t = jnp.pad(w_in_t[l:l + 1, gate0:gate0 + n_gate, :], ((0, 0), (0, 128 - n_gate), (0, 0)))
        w_uv_t = w_in_t[l:l + 1, gate0 + n_gate:, :]
        bias_row = jnp.pad(b_if[l].reshape(1, n_gate), ((0, 0), (0, 128 - n_gate)))

        h = h.reshape(m, d)
        proj = _matmul_stream_nt(h, w_in_t, l, gate0, 1024, 1024, bf16, "in_proj").reshape(bsz, t, gate0)
        proj_uv = _matmul_stream_nt(h, w_uv_t, 0, 2 * sgu_w, 1024, 1024, bf16, "in_proj_uv").reshape(bsz, t, 2 * sgu_w)
        if_pre = _matmul_wcast_nt(h, w_if_t, 0, 128, 1024, 128, jnp.float32, "gate_proj").reshape(bsz, t, 128)

        y_att = _moba_attention(proj, rel_bias, diag_tiles, near_tiles, att_heads)
        g_rows, m_cols, e_cols = _mlstm_gates(if_pre, bias_row)
        y_ml = _mlstm(proj, conv_qk[l], g_rows, m_cols, e_cols, g_mlstm[l], 3 * att_w)
        y_sgu = _sgu(proj_uv, w_sgu[l], b_sgu[l], g_sgu[l], 0)
        y = _matmul3_stream(y_att.reshape(m, att_w), y_ml.reshape(m, ml_w), y_sgu.reshape(m, sgu_w),
                            w_out, l, 1024, 1024, bf16, "out_proj")
        x, h = _postnorm_prenorm(x, y.reshape(bsz, t, d), g_mix_post[l], base_mod, tab, 2,
                                 g_ffn_pre[l], tab, 3, 4)

        act = _swiglu_matmul(h.reshape(m, d), w_gate, w_up, l, 2048, 256)
        y = _matmul_kacc(act, w_down_b, l, 1024, 512, ffn // 2, bf16, "ffn_down").reshape(bsz, t, d)
        if l + 1 < depth:
            x, h = _postnorm_prenorm(x, y, g_ffn_post[l], base_mod, tab, 5,
                                     g_mix_pre[l + 1], ada_table[l + 1], 0, 1)
        else:
            x = _postnorm_residual(x, y, g_ffn_post[l], base_mod, tab, 5)
    return x
```

```python
import functools
import math

import numpy as np
import jax
import jax.numpy as jnp
from jax import lax
from jax.experimental import pallas as pl
from jax.experimental.pallas import tpu as pltpu

ATT_HEAD_DIM = 128
MOBA_BLOCK = 256
MOBA_TOPK = 3
REL_BUCKETS = 32
REL_MAX_DIST = 128
MLSTM_HEADS = 4
MLSTM_BLOCK = 256
CONV_WIDTH = 4
SGU_GROUPS = 8
SGU_CHUNK = 128
N_MOD = 6
RMS_EPS = 1e-6
LN_EPS = 1e-5
MASKED = -1e30
LOG2E = math.log2(math.e)

VMEM_LIMIT_BYTES = 56 * 1024 * 1024


def _cparams(n_axes):
    return pltpu.CompilerParams(dimension_semantics=("arbitrary",) * n_axes,
                                vmem_limit_bytes=VMEM_LIMIT_BYTES)


def _dot(a, b):
    return jnp.dot(a, b, preferred_element_type=jnp.float32)


def _dot_nt(a, b):
    return lax.dot_general(a, b, (((1,), (1,)), ((), ())), preferred_element_type=jnp.float32)


def _dot_tn(a, b):
    return lax.dot_general(a, b, (((0,), (0,)), ((), ())), preferred_element_type=jnp.float32)


def _silu(x):
    return x * jax.nn.sigmoid(x)


def _gelu_tanh(x):
    return 0.5 * x * (1.0 + jnp.tanh(math.sqrt(2.0 / math.pi) * (x + 0.044715 * (x * x * x))))


def _ada_kernel(c_ref, w_ref, b_ref, o_ref):
    a = _silu(c_ref[...]).astype(jnp.bfloat16)
    o_ref[...] = _dot(a, w_ref[...].astype(jnp.bfloat16)) + b_ref[...]


def _ada_base(c, w_ada, b_ada):
    b, d = c.shape
    n = w_ada.shape[1]
    rows = 16
    c_pad = jnp.zeros((rows, d), jnp.float32).at[:b].set(c)
    tn = 1024
    out = pl.pallas_call(
        _ada_kernel,
        grid=(n // tn,),
        in_specs=[pl.BlockSpec((rows, d), lambda j: (0, 0)),
                  pl.BlockSpec((d, tn), lambda j: (0, j)),
                  pl.BlockSpec((1, tn), lambda j: (0, j))],
        out_specs=pl.BlockSpec((rows, tn), lambda j: (0, j)),
        out_shape=jax.ShapeDtypeStruct((rows, n), jnp.float32),
        compiler_params=_cparams(1),
        name="ada_base",
    )(c_pad, w_ada, b_ada.reshape(1, n))
    return out[:b].reshape(b, N_MOD, n // N_MOD)


def _prenorm_kernel(shift_idx, scale_idx, x_ref, g_ref, mod_ref, tab_ref, o_ref):
    x = x_ref[0]
    y = x * lax.rsqrt(jnp.mean(x * x, axis=-1, keepdims=True) + RMS_EPS) * g_ref[...]
    scale = mod_ref[0, scale_idx:scale_idx + 1, :] + tab_ref[scale_idx:scale_idx + 1, :]
    shift = mod_ref[0, shift_idx:shift_idx + 1, :] + tab_ref[shift_idx:shift_idx + 1, :]
    o_ref[0] = (y * (1.0 + scale) + shift).astype(o_ref.dtype)


def _prenorm(x, g, base_mod, tab, shift_idx, scale_idx):
    b, t, d = x.shape
    tt = 256
    return pl.pallas_call(
        functools.partial(_prenorm_kernel, shift_idx, scale_idx),
        grid=(b, t // tt),
        in_specs=[pl.BlockSpec((1, tt, d), lambda bi, ti: (bi, ti, 0)),
                  pl.BlockSpec((1, d), lambda bi, ti: (0, 0)),
                  pl.BlockSpec((1, N_MOD, d), lambda bi, ti: (bi, 0, 0)),
                  pl.BlockSpec((N_MOD, d), lambda bi, ti: (0, 0))],
        out_specs=pl.BlockSpec((1, tt, d), lambda bi, ti: (bi, ti, 0)),
        out_shape=jax.ShapeDtypeStruct((b, t, d), jnp.bfloat16),
        compiler_params=_cparams(2),
        name="prenorm",
    )(x, g.reshape(1, d), base_mod, tab)


def _postnorm_kernel(gate_idx, x_ref, y_ref, g_ref, mod_ref, tab_ref, o_ref):
    y = y_ref[0].astype(jnp.float32)
    yn = y * lax.rsqrt(jnp.mean(y * y, axis=-1, keepdims=True) + RMS_EPS) * g_ref[...]
    gate = mod_ref[0, gate_idx:gate_idx + 1, :] + tab_ref[gate_idx:gate_idx + 1, :]
    o_ref[0] = x_ref[0] + gate * yn


def _post_pre_kernel(gate_idx, shift_idx, scale_idx, x_ref, y_ref, g_ref, gn_ref, mod_ref, tab_ref, tabn_ref,
                     o_ref, h_ref):
    _postnorm_kernel(gate_idx, x_ref, y_ref, g_ref, mod_ref, tab_ref, o_ref)
    _prenorm_kernel(shift_idx, scale_idx, o_ref, gn_ref, mod_ref, tabn_ref, h_ref)


def _postnorm_prenorm(x, y, g_post, base_mod, tab, gate_idx, g_pre, tab_next, shift_idx, scale_idx):
    b, t, d = x.shape
    tt = 256
    row_spec = pl.BlockSpec((1, tt, d), lambda bi, ti: (bi, ti, 0))
    vec_spec = pl.BlockSpec((1, d), lambda bi, ti: (0, 0))
    tab_spec = pl.BlockSpec((N_MOD, d), lambda bi, ti: (0, 0))
    return pl.pallas_call(
        functools.partial(_post_pre_kernel, gate_idx, shift_idx, scale_idx),
        grid=(b, t // tt),
        in_specs=[row_spec, row_spec, vec_spec, vec_spec,
                  pl.BlockSpec((1, N_MOD, d), lambda bi, ti: (bi, 0, 0)), tab_spec, tab_spec],
        out_specs=[row_spec, row_spec],
        out_shape=[jax.ShapeDtypeStruct((b, t, d), jnp.float32), jax.ShapeDtypeStruct((b, t, d), jnp.bfloat16)],
        compiler_params=_cparams(2),
        name="postnorm_prenorm",
    )(x, y, g_post.reshape(1, d), g_pre.reshape(1, d), base_mod, tab, tab_next)


def _postnorm_residual(x, y, g, base_mod, tab, gate_idx):
    b, t, d = x.shape
    tt = 256
    return pl.pallas_call(
        functools.partial(_postnorm_kernel, gate_idx),
        grid=(b, t // tt),
        in_specs=[pl.BlockSpec((1, tt, d), lambda bi, ti: (bi, ti, 0)),
                  pl.BlockSpec((1, tt, d), lambda bi, ti: (bi, ti, 0)),
                  pl.BlockSpec((1, d), lambda bi, ti: (0, 0)),
                  pl.BlockSpec((1, N_MOD, d), lambda bi, ti: (bi, 0, 0)),
                  pl.BlockSpec((N_MOD, d), lambda bi, ti: (0, 0))],
        out_specs=pl.BlockSpec((1, tt, d), lambda bi, ti: (bi, ti, 0)),
        out_shape=jax.ShapeDtypeStruct((b, t, d), jnp.float32),
        compiler_params=_cparams(2),
        name="postnorm_residual",
    )(x, y, g.reshape(1, d), base_mod, tab)


def _stage_weight_piece(w_ref, wb_ref):
    t, i = pl.program_id(0), pl.program_id(1)
    rows = w_ref.shape[1]
    wb_ref[t % 2, pl.ds(pl.multiple_of(i * rows, rows), rows), :] = w_ref[0].astype(jnp.bfloat16)


def _staged_weight(wb_ref):
    return wb_ref[(pl.program_id(0) + 1) % 2]


def _stream_specs(tm, n_tiles, x_widths, out_tn):
    row = lambda t, i: jnp.where(t > 0, i, 0)
    x_specs = [pl.BlockSpec((tm, kx), lambda t, i: (row(t, i), 0)) for kx in x_widths]
    out_spec = pl.BlockSpec((tm, out_tn), lambda t, i: (row(t, i), jnp.maximum(t - 1, 0)))
    return x_specs, out_spec


def _mm_stream_nt_kernel(x_ref, w_ref, o_ref, wb_ref):
    _stage_weight_piece(w_ref, wb_ref)

    @pl.when(pl.program_id(0) > 0)
    def _():
        o_ref[...] = _dot_nt(x_ref[...], _staged_weight(wb_ref)).astype(o_ref.dtype)


def _matmul_stream_nt(x, wt_stack, layer, n_cols, tm, tn, out_dtype, name):
    m, k = x.shape
    n_tiles, n_rows = n_cols // tn, m // tm
    piece = tn // n_rows
    x_specs, out_spec = _stream_specs(tm, n_tiles, [k], tn)
    return pl.pallas_call(
        _mm_stream_nt_kernel,
        grid=(n_tiles + 1, n_rows),
        in_specs=x_specs + [pl.BlockSpec(
            (1, piece, k), lambda t, i: (layer, jnp.minimum(t, n_tiles - 1) * n_rows + i, 0))],
        out_specs=out_spec,
        out_shape=jax.ShapeDtypeStruct((m, n_cols), out_dtype),
        scratch_shapes=[pltpu.VMEM((2, tn, k), jnp.bfloat16)],
        compiler_params=_cparams(2),
        name=name,
    )(x, wt_stack)


def _cast_weight_once(w_ref, wb_ref):
    @pl.when(pl.program_id(1) == 0)
    def _():
        wb_ref[...] = w_ref[0].astype(jnp.bfloat16)


def _mm_wcast_nt_kernel(x_ref, w_ref, o_ref, wb_ref):
    _cast_weight_once(w_ref, wb_ref)
    o_ref[...] = _dot_nt(x_ref[...], wb_ref[...]).astype(o_ref.dtype)


def _matmul_wcast_nt(x, wt_stack, layer, n_cols, tm, tn, out_dtype, name):
    m, k = x.shape
    return pl.pallas_call(
        _mm_wcast_nt_kernel,
        grid=(n_cols // tn, m // tm),
        in_specs=[pl.BlockSpec((tm, k), lambda j, i: (i, 0)),
                  pl.BlockSpec((1, tn, k), lambda j, i: (layer, j, 0))],
        out_specs=pl.BlockSpec((tm, tn), lambda j, i: (i, j)),
        out_shape=jax.ShapeDtypeStruct((m, n_cols), out_dtype),
        scratch_shapes=[pltpu.VMEM((tn, k), jnp.bfloat16)],
        compiler_params=_cparams(2),
        name=name,
    )(x, wt_stack)


def _stream_weight_spec(w_stack, layer, tn, n_rows):
    k, n = w_stack.shape[1], w_stack.shape[2]
    n_tiles = n // tn
    return pl.BlockSpec((1, k // n_rows, tn), lambda t, i: (layer, i, jnp.minimum(t, n_tiles - 1)))


def _mm3_stream_kernel(xa_ref, xb_ref, xc_ref, w_ref, o_ref, wb_ref):
    _stage_weight_piece(w_ref, wb_ref)

    @pl.when(pl.program_id(0) > 0)
    def _():
        ka, kb = xa_ref.shape[1], xb_ref.shape[1]
        w = _staged_weight(wb_ref)
        o_ref[...] = (_dot(xa_ref[...], w[0:ka, :]) + _dot(xb_ref[...], w[ka:ka + kb, :])
                      + _dot(xc_ref[...], w[ka + kb:, :])).astype(o_ref.dtype)


def _matmul3_stream(xa, xb, xc, w_stack, layer, tm, tn, out_dtype, name):
    m = xa.shape[0]
    k, n = w_stack.shape[1], w_stack.shape[2]
    n_tiles, n_rows = n // tn, m // tm
    x_specs, out_spec = _stream_specs(tm, n_tiles, [xa.shape[1], xb.shape[1], xc.shape[1]], tn)
    return pl.pallas_call(
        _mm3_stream_kernel,
        grid=(n_tiles + 1, n_rows),
        in_specs=x_specs + [_stream_weight_spec(w_stack, layer, tn, n_rows)],
        out_specs=out_spec,
        out_shape=jax.ShapeDtypeStruct((m, n), out_dtype),
        scratch_shapes=[pltpu.VMEM((2, k, tn), jnp.bfloat16)],
        compiler_params=_cparams(2),
        name=name,
    )(xa, xb, xc, w_stack)


def _swiglu_stream_kernel(x_ref, wg_ref, wu_ref, wd_ref, o_ref, wdb_ref, wgb_ref, wub_ref):
    _stage_weight_piece(wg_ref, wgb_ref)
    _stage_weight_piece(wu_ref, wub_ref)
    wdb_ref[...] = wd_ref[0].astype(jnp.bfloat16)

    @pl.when(pl.program_id(0) > 0)
    def _():
        x = x_ref[...]
        gate = _dot(x, _staged_weight(wgb_ref))
        o_ref[...] = (_silu(gate) * _dot(x, _staged_weight(wub_ref))).astype(o_ref.dtype)


def _swiglu_matmul(x, wg_stack, wu_stack, wd_stack, layer, tm, tn):
    m, k = x.shape
    n = wg_stack.shape[2]
    n_tiles, n_rows = n // tn, m // tm
    slab = wd_stack.shape[1] // (n_tiles * n_rows)
    assert slab * n_tiles * n_rows == wd_stack.shape[1]
    slab_index = lambda t, i: jnp.maximum((t - 1) * n_rows + i, 0)
    x_specs, out_spec = _stream_specs(tm, n_tiles, [k], tn)
    w_spec = _stream_weight_spec(wg_stack, layer, tn, n_rows)
    return pl.pallas_call(
        _swiglu_stream_kernel,
        grid=(n_tiles + 1, n_rows),
        in_specs=x_specs + [w_spec, w_spec,
                            pl.BlockSpec((1, slab, wd_stack.shape[2]), lambda t, i: (layer, slab_index(t, i), 0))],
        out_specs=[out_spec, pl.BlockSpec((slab, wd_stack.shape[2]), lambda t, i: (slab_index(t, i), 0))],
        out_shape=[jax.ShapeDtypeStruct((m, n), jnp.bfloat16),
                   jax.ShapeDtypeStruct(wd_stack.shape[1:], jnp.bfloat16)],
        scratch_shapes=[pltpu.VMEM((2, k, tn), jnp.bfloat16), pltpu.VMEM((2, k, tn), jnp.bfloat16)],
        compiler_params=_cparams(2),
        name="ffn_gate_up",
    )(x, wg_stack, wu_stack, wd_stack)


def _mm_rows_kernel(x_ref, w_ref, o_ref):
    o_ref[...] = _dot(x_ref[...], w_ref[...]).astype(o_ref.dtype)


def _matmul_rows(x, w, tm, tn, out_dtype, name):
    m, k = x.shape
    n = w.shape[1]
    return pl.pallas_call(
        _mm_rows_kernel,
        grid=(m // tm, n // tn),
        in_specs=[pl.BlockSpec((tm, k), lambda i, j: (i, 0)),
                  pl.BlockSpec((k, tn), lambda i, j: (0, j))],
        out_specs=pl.BlockSpec((tm, tn), lambda i, j: (i, j)),
        out_shape=jax.ShapeDtypeStruct((m, n), out_dtype),
        compiler_params=_cparams(2),
        name=name,
    )(x, w)


def _mm_kacc_kernel(x_ref, w_ref, o_ref, acc_ref):
    kk = pl.program_id(2)

    @pl.when(kk == 0)
    def _():
        acc_ref[...] = jnp.zeros_like(acc_ref)

    acc_ref[...] += _dot(x_ref[...], w_ref[0])

    @pl.when(kk == pl.num_programs(2) - 1)
    def _():
        o_ref[...] = acc_ref[...].astype(o_ref.dtype)


def _matmul_kacc(x, w_stack, layer, tm, tn, tk, out_dtype, name):
    m, k = x.shape
    n = w_stack.shape[2]
    return pl.pallas_call(
        _mm_kacc_kernel,
        grid=(m // tm, n // tn, k // tk),
        in_specs=[pl.BlockSpec((tm, tk), lambda i, j, kk: (i, kk)),
                  pl.BlockSpec((1, tk, tn), lambda i, j, kk: (layer, kk, j))],
        out_specs=pl.BlockSpec((tm, tn), lambda i, j, kk: (i, j)),
        out_shape=jax.ShapeDtypeStruct((m, n), out_dtype),
        scratch_shapes=[pltpu.VMEM((tm, tn), jnp.float32)],
        compiler_params=_cparams(3),
        name=name,
    )(x, w_stack)


def _rel_bucket_starts():
    max_exact = REL_BUCKETS // 2
    d = np.arange(0, 4 * REL_MAX_DIST)
    ratio = np.maximum(d, 1).astype(np.float32) / np.float32(max_exact)
    large = max_exact + (np.log(ratio) / np.float32(math.log(REL_MAX_DIST / max_exact))
                         * np.float32(REL_BUCKETS - max_exact)).astype(np.int32)
    bucket = np.where(d < max_exact, d, np.minimum(large, REL_BUCKETS - 1))
    assert np.all(np.diff(bucket) >= 0)
    return [int(np.argmax(bucket == j)) for j in range(REL_BUCKETS)]


def _rel_tiles_kernel(starts, rel_ref, diag_ref, near_ref):
    h = pl.program_id(0)
    qry = lax.broadcasted_iota(jnp.int32, (MOBA_BLOCK, MOBA_BLOCK), 0)
    key = lax.broadcasted_iota(jnp.int32, (MOBA_BLOCK, MOBA_BLOCK), 1)
    dist = qry - key

    def bias_of(d):
        val = jnp.full(d.shape, rel_ref[0, h], jnp.float32)
        for j in range(1, REL_BUCKETS):
            val = jnp.where(d >= starts[j], rel_ref[j, h], val)
        return val

    diag_ref[0] = jnp.where(dist >= 0, bias_of(dist) * LOG2E, MASKED)
    near_ref[0] = bias_of(dist + MOBA_BLOCK) * LOG2E


def _rel_bias_tiles(rel_bias):
    heads = rel_bias.shape[1]
    tile = jax.ShapeDtypeStruct((heads, MOBA_BLOCK, MOBA_BLOCK), jnp.float32)
    spec = pl.BlockSpec((1, MOBA_BLOCK, MOBA_BLOCK), lambda h: (h, 0, 0))
    return pl.pallas_call(
        functools.partial(_rel_tiles_kernel, _rel_bucket_starts()),
        grid=(heads,),
        in_specs=[pl.BlockSpec(memory_space=pltpu.SMEM)],
        out_specs=[spec, spec],
        out_shape=[tile, tile],
        compiler_params=_cparams(1),
        name="rel_bias_tiles",
    )(rel_bias)


def _moba_kernel(n_blocks, rel_ref, q_ref, k_ref, v_ref, diag_ref, near_ref, o_ref,
                 kaug_ref, qaug_ref, s_ref, p_ref, l_ref, acc_ref):
    h = pl.program_id(1)
    blk, dh = MOBA_BLOCK, ATT_HEAD_DIM
    t = n_blocks * blk
    scale = dh ** -0.5 * LOG2E
    far_bias = rel_ref[REL_BUCKETS - 1, h] * LOG2E
    k = k_ref[0]

    row = lax.broadcasted_iota(jnp.int32, (t, dh), 0)
    col = lax.broadcasted_iota(jnp.int32, (t, dh), 1)
    lo = col * blk
    kaug_ref[:, 0:dh] = k
    kaug_ref[:, dh:2 * dh] = jnp.where((row >= lo) & (row < lo + blk), 1.0, 0.0).astype(jnp.bfloat16)

    brow = lax.broadcasted_iota(jnp.int32, (n_blocks, t), 0) * blk
    bcol = lax.broadcasted_iota(jnp.int32, (n_blocks, t), 1)
    avg = jnp.where((bcol >= brow) & (bcol < brow + blk), 1.0 / blk, 0.0).astype(jnp.bfloat16)
    kmean = _dot(avg, k)
    km_hi = kmean.astype(jnp.bfloat16)
    km_lo = (kmean - km_hi.astype(jnp.float32)).astype(jnp.bfloat16)

    q = q_ref[0]
    s_gate = _dot_nt(km_hi, q) + _dot_nt(km_lo, q)
    bid = lax.broadcasted_iota(jnp.int32, s_gate.shape, 0)
    past = (bid + 1) * blk <= lax.broadcasted_iota(jnp.int32, s_gate.shape, 1)
    s_gate = jnp.where(past, s_gate, -jnp.inf)
    rank = jnp.zeros(s_gate.shape, jnp.int32)
    for j in range(n_blocks):
        s_j = s_gate[j:j + 1, :]
        beats = (s_j > s_gate) | ((s_j == s_gate) & (j < bid))
        rank = rank + jnp.where(beats, 1, 0)
    offs = jnp.where(past & (rank >= MOBA_TOPK), MASKED, 0.0)
    offs = jnp.concatenate([offs, jnp.zeros((dh - n_blocks, t), jnp.float32)], axis=0)
    qaug_ref[:, 0:dh] = q
    qaug_ref[:, dh:2 * dh] = offs.T.astype(jnp.bfloat16)

    for j in range(n_blocks):
        cols = slice(j * blk, (j + 1) * blk)
        s_ref[j * blk:, cols] = _dot_nt(qaug_ref[j * blk:, :], kaug_ref[cols, :])

    for i in range(n_blocks):
        rows = slice(i * blk, (i + 1) * blk)
        n_keys = (i + 1) * blk
        s = s_ref[rows, 0:n_keys] * scale
        parts = []
        if i >= 2:
            parts.append(s[:, :(i - 1) * blk] + far_bias)
        if i >= 1:
            parts.append(s[:, (i - 1) * blk:i * blk] + near_ref[0])
        parts.append(s[:, i * blk:] + diag_ref[0])
        s = jnp.concatenate(parts, axis=1) if len(parts) > 1 else parts[0]
        p = jnp.exp2(s - jnp.max(s, axis=1, keepdims=True))
        l_ref[rows, :] = jnp.sum(p, axis=1, keepdims=True)
        p_ref[rows, 0:n_keys] = p.astype(jnp.bfloat16)

    for j in range(n_blocks):
        cols = slice(j * blk, (j + 1) * blk)
        pv = _dot(p_ref[j * blk:, cols], v_ref[0, cols, :])
        if j == 0:
            acc_ref[...] = pv
        else:
            acc_ref[j * blk:, :] += pv
    o_ref[0] = (acc_ref[...] / l_ref[...]).astype(o_ref.dtype)


def _moba_attention(proj, rel_bias, diag_tiles, near_tiles, heads):
    b, t, _ = proj.shape
    blk, dh = MOBA_BLOCK, ATT_HEAD_DIM
    n_blocks = t // blk
    assert n_blocks <= dh
    tile_spec = pl.BlockSpec((1, blk, blk), lambda bi, h: (h, 0, 0))
    seq_spec = lambda off: pl.BlockSpec((1, t, dh), lambda bi, h: (bi, 0, off * heads + h))
    return pl.pallas_call(
        functools.partial(_moba_kernel, n_blocks),
        grid=(b, heads),
        in_specs=[pl.BlockSpec(memory_space=pltpu.SMEM), seq_spec(0), seq_spec(1), seq_spec(2),
                  tile_spec, tile_spec],
        out_specs=pl.BlockSpec((1, t, dh), lambda bi, h: (bi, 0, h)),
        out_shape=jax.ShapeDtypeStruct((b, t, heads * dh), jnp.bfloat16),
        scratch_shapes=[pltpu.VMEM((t, 2 * dh), jnp.bfloat16),
                        pltpu.VMEM((t, 2 * dh), jnp.bfloat16),
                        pltpu.VMEM((t, t), jnp.float32),
                        pltpu.VMEM((t, t), jnp.bfloat16),
                        pltpu.VMEM((t, 1), jnp.float32),
                        pltpu.VMEM((t, dh), jnp.float32)],
        compiler_params=_cparams(2),
        name="moba_attention",
    )(rel_bias, proj, proj, proj, diag_tiles, near_tiles)


def _scan_rows(x, op, identity):
    n = x.shape[0]
    row = lax.broadcasted_iota(jnp.int32, x.shape, 0)
    s = 1
    while s < n:
        x = op(x, jnp.where(row >= s, pltpu.roll(x, s, axis=0), identity))
        s *= 2
    return x


def _mlstm_gates_kernel(if_ref, bias_ref, g_ref, mb_ref, eb_ref):
    pre = if_ref[0] + bias_ref[...]
    lanes = pre.shape[1]
    i_pre = pre
    f_pre = pltpu.roll(pre, lanes - MLSTM_HEADS, axis=1)
    log_f = -(jnp.maximum(-f_pre, 0.0) + jnp.log1p(jnp.exp(-jnp.abs(f_pre))))
    b = _scan_rows(log_f, jnp.add, 0.0)
    g = i_pre - b
    m_rel = jnp.maximum(_scan_rows(g, jnp.maximum, -jnp.inf), 0.0)
    e = jnp.exp(-(b + m_rel))
    g_t = g.T
    t = pre.shape[0]
    for hh in range(MLSTM_HEADS):
        g_ref[0, hh] = g_t[hh:hh + 1, :]
        mb_ref[0, hh] = jnp.broadcast_to(m_rel[:, hh:hh + 1], (t, lanes))
        eb_ref[0, hh] = jnp.broadcast_to(e[:, hh:hh + 1], (t, lanes))


def _mlstm_gates(if_pre, bias_row):
    b, t, lanes = if_pre.shape
    col = jax.ShapeDtypeStruct((b, MLSTM_HEADS, t, lanes), jnp.float32)
    col_spec = pl.BlockSpec((1, MLSTM_HEADS, t, lanes), lambda bi: (bi, 0, 0, 0))
    return pl.pallas_call(
        _mlstm_gates_kernel,
        grid=(b,),
        in_specs=[pl.BlockSpec((1, t, lanes), lambda bi: (bi, 0, 0)),
                  pl.BlockSpec((1, lanes), lambda bi: (0, 0))],
        out_specs=[pl.BlockSpec((1, MLSTM_HEADS, 1, t), lambda bi: (bi, 0, 0, 0)), col_spec, col_spec],
        out_shape=[jax.ShapeDtypeStruct((b, MLSTM_HEADS, 1, t), jnp.float32), col, col],
        compiler_params=_cparams(1),
        name="mlstm_gates",
    )(if_pre, bias_row)


def _causal_conv_silu(x, w):
    row = lax.broadcasted_iota(jnp.int32, x.shape, 0)
    out = x * w[CONV_WIDTH - 1:CONV_WIDTH, :]
    for back in range(1, CONV_WIDTH):
        shifted = jnp.where(row >= back, pltpu.roll(x, back, axis=0), 0.0)
        out = out + shifted * w[CONV_WIDTH - 1 - back:CONV_WIDTH - back, :]
    return _silu(out)


def _mlstm_kernel(q_ref, k_ref, v_ref, o_ref, wq_ref, wk_ref, g_ref, mb_ref, eb_ref, gn_ref, y_ref,
                  qc_ref, kc_ref):
    blk = MLSTM_BLOCK
    t, d = q_ref.shape[1], q_ref.shape[2]
    lanes = mb_ref.shape[3]
    qc_ref[...] = _causal_conv_silu(q_ref[0].astype(jnp.float32), wq_ref[...]).astype(jnp.bfloat16)
    kc = _causal_conv_silu(k_ref[0].astype(jnp.float32), wk_ref[...]) * (d ** -0.5)
    kc_ref[...] = kc.astype(jnp.bfloat16)
    qi = lax.broadcasted_iota(jnp.int32, (blk, blk), 0)
    ki = lax.broadcasted_iota(jnp.int32, (blk, blk), 1)

    for i in range(t // blk):
        rows = slice(i * blk, (i + 1) * blk)
        n_keys = (i + 1) * blk
        qk = _dot_nt(qc_ref[rows, :], kc_ref[0:n_keys, :])
        m_col = mb_ref[0, 0, rows, :]
        m_full = jnp.concatenate([m_col] * (n_keys // lanes), axis=1)
        w = jnp.exp(g_ref[0, 0, :, 0:n_keys] - m_full)
        w_own = jnp.where(ki <= qi, w[:, i * blk:], 0.0)
        w = jnp.concatenate([w[:, :i * blk], w_own], axis=1) if i else w_own
        s = qk * w
        den = jnp.sum(s, axis=-1, keepdims=True)
        num = _dot(s.astype(jnp.bfloat16), v_ref[0, 0:n_keys, :])
        hid = num / jnp.maximum(jnp.abs(den), eb_ref[0, 0, rows, 0:1])
        hn = hid * lax.rsqrt(jnp.mean(hid * hid, axis=-1, keepdims=True) + RMS_EPS) * gn_ref[...]
        y_ref[0, rows, :] = (jax.nn.sigmoid(o_ref[0, rows, :].astype(jnp.float32)) * hn).astype(y_ref.dtype)


def _mlstm(proj, conv_w, g_rows, m_cols, e_cols, g_norm, col0):
    b, t, _ = proj.shape
    heads = MLSTM_HEADS
    d = g_norm.shape[0] // heads
    c0 = col0 // d
    lanes = m_cols.shape[3]
    full = lambda off: pl.BlockSpec((1, t, d), lambda bi, h: (bi, 0, c0 + off * heads + h))
    col_spec = pl.BlockSpec((1, 1, t, lanes), lambda bi, h: (bi, h, 0, 0))
    return pl.pallas_call(
        _mlstm_kernel,
        grid=(b, heads),
        in_specs=[full(0), full(1), full(2), full(3),
                  pl.BlockSpec((CONV_WIDTH, d), lambda bi, h: (0, h)),
                  pl.BlockSpec((CONV_WIDTH, d), lambda bi, h: (0, heads + h)),
                  pl.BlockSpec((1, 1, 1, t), lambda bi, h: (bi, h, 0, 0)),
                  col_spec, col_spec,
                  pl.BlockSpec((1, d), lambda bi, h: (0, h))],
        out_specs=pl.BlockSpec((1, t, d), lambda bi, h: (bi, 0, h)),
        out_shape=jax.ShapeDtypeStruct((b, t, heads * d), jnp.bfloat16),
        scratch_shapes=[pltpu.VMEM((t, d), jnp.bfloat16), pltpu.VMEM((t, d), jnp.bfloat16)],
        compiler_params=_cparams(2),
        name="mlstm",
    )(proj, proj, proj, proj, conv_w, conv_w, g_rows, m_cols, e_cols, g_norm.reshape(1, -1))


def _sgu_kernel(u_ref, v_ref, w_ref, bt_ref, g_ref, y_ref):
    chunk = SGU_CHUNK
    tt, width = u_ref.shape[1], u_ref.shape[2]
    gd = width // SGU_GROUPS
    v = _gelu_tanh(v_ref[0].astype(jnp.float32))
    vc = v - jnp.mean(v, axis=-1, keepdims=True)
    var = jnp.mean(vc * vc, axis=-1, keepdims=True)
    vn = (vc * lax.rsqrt(var + LN_EPS) * g_ref[...]).astype(jnp.bfloat16)
    row = lax.broadcasted_iota(jnp.int32, (chunk, chunk), 0)
    col = lax.broadcasted_iota(jnp.int32, (chunk, chunk), 1)
    for g in range(SGU_GROUPS):
        w = jnp.where(col <= row, w_ref[g], 0.0).astype(jnp.bfloat16)
        bias = bt_ref[:, g:g + 1]
        cols = slice(g * gd, (g + 1) * gd)
        for n in range(tt // chunk):
            rows = slice(n * chunk, (n + 1) * chunk)
            mixed = _dot(w, vn[rows, cols]) + bias
            u = _gelu_tanh(u_ref[0, rows, cols].astype(jnp.float32))
            y_ref[0, rows, cols] = (u * mixed).astype(y_ref.dtype)


def _sgu(proj, w_s, b_s, g_v, col0):
    b, t, _ = proj.shape
    width = g_v.shape[0]
    tt = 512
    c0 = col0 // width
    return pl.pallas_call(
        _sgu_kernel,
        grid=(b, t // tt),
        in_specs=[pl.BlockSpec((1, tt, width), lambda bi, ti: (bi, ti, c0)),
                  pl.BlockSpec((1, tt, width), lambda bi, ti: (bi, ti, c0 + 1)),
                  pl.BlockSpec(w_s.shape, lambda bi, ti: (0, 0, 0)),
                  pl.BlockSpec((SGU_CHUNK, SGU_GROUPS), lambda bi, ti: (0, 0)),
                  pl.BlockSpec((1, width), lambda bi, ti: (0, 0))],
        out_specs=pl.BlockSpec((1, tt, width), lambda bi, ti: (bi, ti, 0)),
        out_shape=jax.ShapeDtypeStruct((b, t, width), jnp.bfloat16),
        compiler_params=_cparams(2),
        name="sgu",
    )(proj, proj, w_s, b_s.T, g_v.reshape(1, width))


def kernel(x, c, w_ada, b_ada, ada_table, rel_bias, g_mix_pre, g_mix_post, g_ffn_pre, g_ffn_post,
           w_in, conv_qk, b_if, g_mlstm, w_sgu, b_sgu, g_sgu, w_out, w_gate, w_up, w_down):
    bsz, t, d = x.shape
    depth = w_in.shape[0]
    bf16 = jnp.bfloat16
    att_heads = rel_bias.shape[1]
    att_w = att_heads * ATT_HEAD_DIM
    ml_w = g_mlstm.shape[1]
    sgu_w = g_sgu.shape[1]
    n_gate = 2 * MLSTM_HEADS
    gate0 = 3 * att_w + 4 * ml_w
    ffn = w_gate.shape[2]
    m = bsz * t

    base_mod = _ada_base(c, w_ada, b_ada)
    diag_tiles, near_tiles = _rel_bias_tiles(rel_bias)
    w_in_t = jnp.swapaxes(w_in, 1, 2)

    h = _prenorm(x, g_mix_pre[0], base_mod, ada_table[0], 0, 1)
    for l in range(depth):
        tab = ada_table[l]
        w_if_t = jnp.pad(w_in_t[l:l + 1, gate0:gate0 + n_gate, :], ((0, 0), (0, 128 - n_gate), (0, 0)))
        w_uv_t = w_in_t[l:l + 1, gate0 + n_gate:, :]
        bias_row = jnp.pad(b_if[l].reshape(1, n_gate), ((0, 0), (0, 128 - n_gate)))

        h = h.reshape(m, d)
        proj = _matmul_stream_nt(h, w_in_t, l, gate0, 1024, 1024, bf16, "in_proj").reshape(bsz, t, gate0)
        proj_uv = _matmul_stream_nt(h, w_uv_t, 0, 2 * sgu_w, 1024, 1024, bf16, "in_proj_uv").reshape(bsz, t, 2 * sgu_w)
        if_pre = _matmul_wcast_nt(h, w_if_t, 0, 128, 1024, 128, jnp.float32, "gate_proj").reshape(bsz, t, 128)

        y_att = _moba_attention(proj, rel_bias, diag_tiles, near_tiles, att_heads)
        g_rows, m_cols, e_cols = _mlstm_gates(if_pre, bias_row)
        y_ml = _mlstm(proj, conv_qk[l], g_rows, m_cols, e_cols, g_mlstm[l], 3 * att_w)
        y_sgu = _sgu(proj_uv, w_sgu[l], b_sgu[l], g_sgu[l], 0)
        y = _matmul3_stream(y_att.reshape(m, att_w), y_ml.reshape(m, ml_w), y_sgu.reshape(m, sgu_w),
                            w_out, l, 1024, 1024, bf16, "out_proj")
        x, h = _postnorm_prenorm(x, y.reshape(bsz, t, d), g_mix_post[l], base_mod, tab, 2,
                                 g_ffn_pre[l], tab, 3, 4)

        act, w_down_b = _swiglu_matmul(h.reshape(m, d), w_gate, w_up, w_down, l, 2048, 256)
        y = _matmul_rows(act, w_down_b, 512, 512, bf16, "ffn_down").reshape(bsz, t, d)
        if l + 1 < depth:
            x, h = _postnorm_prenorm(x, y, g_ffn_post[l], base_mod, tab, 5,
                                     g_mix_pre[l + 1], ada_table[l + 1], 0, 1)
        else:
            x = _postnorm_residual(x, y, g_ffn_post[l], base_mod, tab, 5)
    return x
```

```python
import functools
import math

import numpy as np
import jax
import jax.numpy as jnp
from jax import lax
from jax.experimental import pallas as pl
from jax.experimental.pallas import tpu as pltpu

ATT_HEAD_DIM = 128
MOBA_BLOCK = 256
MOBA_TOPK = 3
REL_BUCKETS = 32
REL_MAX_DIST = 128
MLSTM_HEADS = 4
MLSTM_BLOCK = 256
CONV_WIDTH = 4
SGU_GROUPS = 8
SGU_CHUNK = 128
N_MOD = 6
RMS_EPS = 1e-6
LN_EPS = 1e-5
MASKED = -1e30
LOG2E = math.log2(math.e)

VMEM_LIMIT_BYTES = 56 * 1024 * 1024


def _cparams(n_axes):
    return pltpu.CompilerParams(dimension_semantics=("arbitrary",) * n_axes,
                                vmem_limit_bytes=VMEM_LIMIT_BYTES)


def _dot(a, b):
    return jnp.dot(a, b, preferred_element_type=jnp.float32)


def _dot_nt(a, b):
    return lax.dot_general(a, b, (((1,), (1,)), ((), ())), preferred_element_type=jnp.float32)


def _silu(x):
    return x * jax.nn.sigmoid(x)


def _gelu_tanh(x):
    return 0.5 * x * (1.0 + jnp.tanh(math.sqrt(2.0 / math.pi) * (x + 0.044715 * (x * x * x))))


def _ada_kernel(c_ref, w_ref, b_ref, o_ref):
    a = _silu(c_ref[...]).astype(jnp.bfloat16)
    o_ref[...] = _dot(a, w_ref[...].astype(jnp.bfloat16)) + b_ref[...]


def _ada_base(c, w_ada, b_ada):
    b, d = c.shape
    n = w_ada.shape[1]
    rows = 16
    c_pad = jnp.zeros((rows, d), jnp.float32).at[:b].set(c)
    tn = 1024
    out = pl.pallas_call(
        _ada_kernel,
        grid=(n // tn,),
        in_specs=[pl.BlockSpec((rows, d), lambda j: (0, 0)),
                  pl.BlockSpec((d, tn), lambda j: (0, j)),
                  pl.BlockSpec((1, tn), lambda j: (0, j))],
        out_specs=pl.BlockSpec((rows, tn), lambda j: (0, j)),
        out_shape=jax.ShapeDtypeStruct((rows, n), jnp.float32),
        compiler_params=_cparams(1),
        name="ada_base",
    )(c_pad, w_ada, b_ada.reshape(1, n))
    return out[:b].reshape(b, N_MOD, n // N_MOD)


def _prenorm_kernel(shift_idx, scale_idx, x_ref, g_ref, mod_ref, tab_ref, o_ref):
    x = x_ref[0]
    y = x * lax.rsqrt(jnp.mean(x * x, axis=-1, keepdims=True) + RMS_EPS) * g_ref[...]
    scale = mod_ref[0, scale_idx:scale_idx + 1, :] + tab_ref[scale_idx:scale_idx + 1, :]
    shift = mod_ref[0, shift_idx:shift_idx + 1, :] + tab_ref[shift_idx:shift_idx + 1, :]
    o_ref[0] = (y * (1.0 + scale) + shift).astype(o_ref.dtype)


def _prenorm(x, g, base_mod, tab, shift_idx, scale_idx):
    b, t, d = x.shape
    tt = 256
    return pl.pallas_call(
        functools.partial(_prenorm_kernel, shift_idx, scale_idx),
        grid=(b, t // tt),
        in_specs=[pl.BlockSpec((1, tt, d), lambda bi, ti: (bi, ti, 0)),
                  pl.BlockSpec((1, d), lambda bi, ti: (0, 0)),
                  pl.BlockSpec((1, N_MOD, d), lambda bi, ti: (bi, 0, 0)),
                  pl.BlockSpec((N_MOD, d), lambda bi, ti: (0, 0))],
        out_specs=pl.BlockSpec((1, tt, d), lambda bi, ti: (bi, ti, 0)),
        out_shape=jax.ShapeDtypeStruct((b, t, d), jnp.bfloat16),
        compiler_params=_cparams(2),
        name="prenorm",
    )(x, g.reshape(1, d), base_mod, tab)


def _postnorm_kernel(gate_idx, x_ref, y_ref, g_ref, mod_ref, tab_ref, o_ref):
    y = y_ref[0].astype(jnp.float32)
    yn = y * lax.rsqrt(jnp.mean(y * y, axis=-1, keepdims=True) + RMS_EPS) * g_ref[...]
    gate = mod_ref[0, gate_idx:gate_idx + 1, :] + tab_ref[gate_idx:gate_idx + 1, :]
    o_ref[0] = x_ref[0] + gate * yn


def _post_pre_kernel(gate_idx, shift_idx, scale_idx, x_ref, y_ref, g_ref, gn_ref, mod_ref, tab_ref, tabn_ref,
                     o_ref, h_ref):
    _postnorm_kernel(gate_idx, x_ref, y_ref, g_ref, mod_ref, tab_ref, o_ref)
    _prenorm_kernel(shift_idx, scale_idx, o_ref, gn_ref, mod_ref, tabn_ref, h_ref)


def _postnorm_prenorm(x, y, g_post, base_mod, tab, gate_idx, g_pre, tab_next, shift_idx, scale_idx):
    b, t, d = x.shape
    tt = 256
    row_spec = pl.BlockSpec((1, tt, d), lambda bi, ti: (bi, ti, 0))
    vec_spec = pl.BlockSpec((1, d), lambda bi, ti: (0, 0))
    tab_spec = pl.BlockSpec((N_MOD, d), lambda bi, ti: (0, 0))
    return pl.pallas_call(
        functools.partial(_post_pre_kernel, gate_idx, shift_idx, scale_idx),
        grid=(b, t // tt),
        in_specs=[row_spec, row_spec, vec_spec, vec_spec,
                  pl.BlockSpec((1, N_MOD, d), lambda bi, ti: (bi, 0, 0)), tab_spec, tab_spec],
        out_specs=[row_spec, row_spec],
        out_shape=[jax.ShapeDtypeStruct((b, t, d), jnp.float32), jax.ShapeDtypeStruct((b, t, d), jnp.bfloat16)],
        compiler_params=_cparams(2),
        name="postnorm_prenorm",
    )(x, y, g_post.reshape(1, d), g_pre.reshape(1, d), base_mod, tab, tab_next)


def _postnorm_residual(x, y, g, base_mod, tab, gate_idx):
    b, t, d = x.shape
    tt = 256
    return pl.pallas_call(
        functools.partial(_postnorm_kernel, gate_idx),
        grid=(b, t // tt),
        in_specs=[pl.BlockSpec((1, tt, d), lambda bi, ti: (bi, ti, 0)),
                  pl.BlockSpec((1, tt, d), lambda bi, ti: (bi, ti, 0)),
                  pl.BlockSpec((1, d), lambda bi, ti: (0, 0)),
                  pl.BlockSpec((1, N_MOD, d), lambda bi, ti: (bi, 0, 0)),
                  pl.BlockSpec((N_MOD, d), lambda bi, ti: (0, 0))],
        out_specs=pl.BlockSpec((1, tt, d), lambda bi, ti: (bi, ti, 0)),
        out_shape=jax.ShapeDtypeStruct((b, t, d), jnp.float32),
        compiler_params=_cparams(2),
        name="postnorm_residual",
    )(x, y, g.reshape(1, d), base_mod, tab)


def _streamed_steps(w_refs, wb_refs, multiply):
    t, i = pl.program_id(0), pl.program_id(1)

    def stage(slot):
        for w_ref, wb_ref in zip(w_refs, wb_refs):
            rows = w_ref.shape[1]
            wb_ref[slot, pl.ds(pl.multiple_of(i * rows, rows), rows), :] = w_ref[0].astype(jnp.bfloat16)

    @pl.when(t == 0)
    def _():
        stage(0)

    for parity in (0, 1):
        @pl.when((t > 0) & (t % 2 == parity))
        def _():
            stage(parity)
            multiply(*[wb_ref[1 - parity] for wb_ref in wb_refs])


def _stream_specs(tm, n_tiles, x_widths, out_tn):
    row = lambda t, i: jnp.where(t > 0, i, 0)
    x_specs = [pl.BlockSpec((tm, kx), lambda t, i: (row(t, i), 0)) for kx in x_widths]
    out_spec = pl.BlockSpec((tm, out_tn), lambda t, i: (row(t, i), jnp.maximum(t - 1, 0)))
    return x_specs, out_spec


def _mm_stream_nt_kernel(x_ref, w_ref, o_ref, wb_ref):
    def multiply(w):
        o_ref[...] = _dot_nt(x_ref[...], w).astype(o_ref.dtype)

    _streamed_steps([w_ref], [wb_ref], multiply)


def _matmul_stream_nt(x, wt_stack, layer, n_cols, tm, tn, out_dtype, name):
    m, k = x.shape
    n_tiles, n_rows = n_cols // tn, m // tm
    piece = tn // n_rows
    x_specs, out_spec = _stream_specs(tm, n_tiles, [k], tn)
    return pl.pallas_call(
        _mm_stream_nt_kernel,
        grid=(n_tiles + 1, n_rows),
        in_specs=x_specs + [pl.BlockSpec(
            (1, piece, k), lambda t, i: (layer, jnp.minimum(t, n_tiles - 1) * n_rows + i, 0))],
        out_specs=out_spec,
        out_shape=jax.ShapeDtypeStruct((m, n_cols), out_dtype),
        scratch_shapes=[pltpu.VMEM((2, tn, k), jnp.bfloat16)],
        compiler_params=_cparams(2),
        name=name,
    )(x, wt_stack)


def _cast_weight_once(w_ref, wb_ref):
    @pl.when(pl.program_id(1) == 0)
    def _():
        wb_ref[...] = w_ref[0].astype(jnp.bfloat16)


def _mm_wcast_nt_kernel(x_ref, w_ref, o_ref, wb_ref):
    _cast_weight_once(w_ref, wb_ref)
    o_ref[...] = _dot_nt(x_ref[...], wb_ref[...]).astype(o_ref.dtype)


def _matmul_wcast_nt(x, wt_stack, layer, n_cols, tm, tn, out_dtype, name):
    m, k = x.shape
    return pl.pallas_call(
        _mm_wcast_nt_kernel,
        grid=(n_cols // tn, m // tm),
        in_specs=[pl.BlockSpec((tm, k), lambda j, i: (i, 0)),
                  pl.BlockSpec((1, tn, k), lambda j, i: (layer, j, 0))],
        out_specs=pl.BlockSpec((tm, tn), lambda j, i: (i, j)),
        out_shape=jax.ShapeDtypeStruct((m, n_cols), out_dtype),
        scratch_shapes=[pltpu.VMEM((tn, k), jnp.bfloat16)],
        compiler_params=_cparams(2),
        name=name,
    )(x, wt_stack)


def _stream_weight_spec(w_stack, layer, tn, n_rows):
    k, n = w_stack.shape[1], w_stack.shape[2]
    n_tiles = n // tn
    return pl.BlockSpec((1, k // n_rows, tn), lambda t, i: (layer, i, jnp.minimum(t, n_tiles - 1)))


def _mm3_stream_kernel(xa_ref, xb_ref, xc_ref, w_ref, o_ref, wb_ref):
    def multiply(w):
        ka, kb = xa_ref.shape[1], xb_ref.shape[1]
        o_ref[...] = (_dot(xa_ref[...], w[0:ka, :]) + _dot(xb_ref[...], w[ka:ka + kb, :])
                      + _dot(xc_ref[...], w[ka + kb:, :])).astype(o_ref.dtype)

    _streamed_steps([w_ref], [wb_ref], multiply)


def _matmul3_stream(xa, xb, xc, w_stack, layer, tm, tn, out_dtype, name):
    m = xa.shape[0]
    k, n = w_stack.shape[1], w_stack.shape[2]
    n_tiles, n_rows = n // tn, m // tm
    x_specs, out_spec = _stream_specs(tm, n_tiles, [xa.shape[1], xb.shape[1], xc.shape[1]], tn)
    return pl.pallas_call(
        _mm3_stream_kernel,
        grid=(n_tiles + 1, n_rows),
        in_specs=x_specs + [_stream_weight_spec(w_stack, layer, tn, n_rows)],
        out_specs=out_spec,
        out_shape=jax.ShapeDtypeStruct((m, n), out_dtype),
        scratch_shapes=[pltpu.VMEM((2, k, tn), jnp.bfloat16)],
        compiler_params=_cparams(2),
        name=name,
    )(xa, xb, xc, w_stack)


def _swiglu_stream_kernel(x_ref, wg_ref, wu_ref, wd_ref, o_ref, wdb_ref, wgb_ref, wub_ref):
    def multiply(wg, wu):
        x = x_ref[...]
        o_ref[...] = (_silu(_dot(x, wg)) * _dot(x, wu)).astype(o_ref.dtype)
        wdb_ref[...] = wd_ref[0].astype(jnp.bfloat16)

    _streamed_steps([wg_ref, wu_ref], [wgb_ref, wub_ref], multiply)


def _swiglu_matmul(x, wg_stack, wu_stack, wd_stack, layer, tm, tn):
    m, k = x.shape
    n = wg_stack.shape[2]
    n_tiles, n_rows = n // tn, m // tm
    slab = wd_stack.shape[1] // (n_tiles * n_rows)
    assert slab * n_tiles * n_rows == wd_stack.shape[1]
    slab_index = lambda t, i: jnp.maximum((t - 1) * n_rows + i, 0)
    x_specs, out_spec = _stream_specs(tm, n_tiles, [k], tn)
    w_spec = _stream_weight_spec(wg_stack, layer, tn, n_rows)
    return pl.pallas_call(
        _swiglu_stream_kernel,
        grid=(n_tiles + 1, n_rows),
        in_specs=x_specs + [w_spec, w_spec,
                            pl.BlockSpec((1, slab, wd_stack.shape[2]), lambda t, i: (layer, slab_index(t, i), 0))],
        out_specs=[out_spec, pl.BlockSpec((slab, wd_stack.shape[2]), lambda t, i: (slab_index(t, i), 0))],
        out_shape=[jax.ShapeDtypeStruct((m, n), jnp.bfloat16),
                   jax.ShapeDtypeStruct(wd_stack.shape[1:], jnp.bfloat16)],
        scratch_shapes=[pltpu.VMEM((2, k, tn), jnp.bfloat16), pltpu.VMEM((2, k, tn), jnp.bfloat16)],
        compiler_params=_cparams(2),
        name="ffn_gate_up",
    )(x, wg_stack, wu_stack, wd_stack)


def _mm_rows_kernel(x_ref, w_ref, o_ref):
    o_ref[...] = _dot(x_ref[...], w_ref[...]).astype(o_ref.dtype)


def _matmul_rows(x, w, tm, tn, out_dtype, name):
    m, k = x.shape
    n = w.shape[1]
    return pl.pallas_call(
        _mm_rows_kernel,
        grid=(m // tm, n // tn),
        in_specs=[pl.BlockSpec((tm, k), lambda i, j: (i, 0)),
                  pl.BlockSpec((k, tn), lambda i, j: (0, j))],
        out_specs=pl.BlockSpec((tm, tn), lambda i, j: (i, j)),
        out_shape=jax.ShapeDtypeStruct((m, n), out_dtype),
        compiler_params=_cparams(2),
        name=name,
    )(x, w)


def _rel_bucket_starts():
    max_exact = REL_BUCKETS // 2
    d = np.arange(0, 4 * REL_MAX_DIST)
    ratio = np.maximum(d, 1).astype(np.float32) / np.float32(max_exact)
    large = max_exact + (np.log(ratio) / np.float32(math.log(REL_MAX_DIST / max_exact))
                         * np.float32(REL_BUCKETS - max_exact)).astype(np.int32)
    bucket = np.where(d < max_exact, d, np.minimum(large, REL_BUCKETS - 1))
    assert np.all(np.diff(bucket) >= 0)
    return [int(np.argmax(bucket == j)) for j in range(REL_BUCKETS)]


def _rel_tiles_kernel(starts, rel_ref, diag_ref, near_ref):
    h = pl.program_id(0)
    qry = lax.broadcasted_iota(jnp.int32, (MOBA_BLOCK, MOBA_BLOCK), 0)
    key = lax.broadcasted_iota(jnp.int32, (MOBA_BLOCK, MOBA_BLOCK), 1)
    dist = qry - key

    def bias_of(d):
        val = jnp.full(d.shape, rel_ref[0, h], jnp.float32)
        for j in range(1, REL_BUCKETS):
            val = jnp.where(d >= starts[j], rel_ref[j, h], val)
        return val

    diag_ref[0] = jnp.where(dist >= 0, bias_of(dist) * LOG2E, MASKED)
    near_ref[0] = bias_of(dist + MOBA_BLOCK) * LOG2E


def _rel_bias_tiles(rel_bias):
    heads = rel_bias.shape[1]
    tile = jax.ShapeDtypeStruct((heads, MOBA_BLOCK, MOBA_BLOCK), jnp.float32)
    spec = pl.BlockSpec((1, MOBA_BLOCK, MOBA_BLOCK), lambda h: (h, 0, 0))
    return pl.pallas_call(
        functools.partial(_rel_tiles_kernel, _rel_bucket_starts()),
        grid=(heads,),
        in_specs=[pl.BlockSpec(memory_space=pltpu.SMEM)],
        out_specs=[spec, spec],
        out_shape=[tile, tile],
        compiler_params=_cparams(1),
        name="rel_bias_tiles",
    )(rel_bias)


def _moba_kernel(n_blocks, rel_ref, q_ref, k_ref, v_ref, diag_ref, near_ref, o_ref,
                 kaug_ref, qaug_ref, s_ref):
    h = pl.program_id(1)
    blk, dh = MOBA_BLOCK, ATT_HEAD_DIM
    t = n_blocks * blk
    scale = dh ** -0.5 * LOG2E
    far_bias = rel_ref[REL_BUCKETS - 1, h] * LOG2E
    k = k_ref[0]

    row = lax.broadcasted_iota(jnp.int32, (t, dh), 0)
    col = lax.broadcasted_iota(jnp.int32, (t, dh), 1)
    lo = col * blk
    kaug_ref[:, 0:dh] = k
    kaug_ref[:, dh:2 * dh] = jnp.where((row >= lo) & (row < lo + blk), 1.0, 0.0).astype(jnp.bfloat16)

    brow = lax.broadcasted_iota(jnp.int32, (n_blocks, t), 0) * blk
    bcol = lax.broadcasted_iota(jnp.int32, (n_blocks, t), 1)
    avg = jnp.where((bcol >= brow) & (bcol < brow + blk), 1.0 / blk, 0.0).astype(jnp.bfloat16)
    kmean = _dot(avg, k)
    km_hi = kmean.astype(jnp.bfloat16)
    km_lo = (kmean - km_hi.astype(jnp.float32)).astype(jnp.bfloat16)

    q = q_ref[0]
    s_gate = _dot_nt(km_hi, q) + _dot_nt(km_lo, q)
    bid = lax.broadcasted_iota(jnp.int32, s_gate.shape, 0)
    past = (bid + 1) * blk <= lax.broadcasted_iota(jnp.int32, s_gate.shape, 1)
    s_gate = jnp.where(past, s_gate, -jnp.inf)
    rank = jnp.zeros(s_gate.shape, jnp.int32)
    for j in range(n_blocks):
        s_j = s_gate[j:j + 1, :]
        beats = (s_j > s_gate) | ((s_j == s_gate) & (j < bid))
        rank = rank + jnp.where(beats, 1, 0)
    offs = jnp.where(past & (rank >= MOBA_TOPK), MASKED, 0.0)
    offs = jnp.concatenate([offs, jnp.zeros((dh - n_blocks, t), jnp.float32)], axis=0)
    qaug_ref[:, 0:dh] = q
    qaug_ref[:, dh:2 * dh] = offs.T.astype(jnp.bfloat16)

    def logits_of_key_block(j):
        cols = slice(j * blk, (j + 1) * blk)
        s_ref[j * blk:, cols] = _dot_nt(qaug_ref[j * blk:, :], kaug_ref[cols, :])

    def attend_query_block(i):
        rows = slice(i * blk, (i + 1) * blk)
        n_keys = (i + 1) * blk
        s = s_ref[rows, 0:n_keys] * scale
        parts = []
        if i >= 2:
            parts.append(s[:, :(i - 1) * blk] + far_bias)
        if i >= 1:
            parts.append(s[:, (i - 1) * blk:i * blk] + near_ref[0])
        parts.append(s[:, i * blk:] + diag_ref[0])
        s = jnp.concatenate(parts, axis=1) if len(parts) > 1 else parts[0]
        p = jnp.exp2(s - jnp.max(s, axis=1, keepdims=True))
        denom = jnp.sum(p, axis=1, keepdims=True)
        out = _dot(p.astype(jnp.bfloat16), v_ref[0, 0:n_keys, :]) / denom
        o_ref[0, rows, :] = out.astype(o_ref.dtype)

    logits_of_key_block(0)
    for i in range(n_blocks):
        if i + 1 < n_blocks:
            logits_of_key_block(i + 1)
        attend_query_block(i)


def _moba_attention(proj, rel_bias, diag_tiles, near_tiles, heads):
    b, t, _ = proj.shape
    blk, dh = MOBA_BLOCK, ATT_HEAD_DIM
    n_blocks = t // blk
    assert n_blocks <= dh
    tile_spec = pl.BlockSpec((1, blk, blk), lambda bi, h: (h, 0, 0))
    seq_spec = lambda off: pl.BlockSpec((1, t, dh), lambda bi, h: (bi, 0, off * heads + h))
    return pl.pallas_call(
        functools.partial(_moba_kernel, n_blocks),
        grid=(b, heads),
        in_specs=[pl.BlockSpec(memory_space=pltpu.SMEM), seq_spec(0), seq_spec(1), seq_spec(2),
                  tile_spec, tile_spec],
        out_specs=pl.BlockSpec((1, t, dh), lambda bi, h: (bi, 0, h)),
        out_shape=jax.ShapeDtypeStruct((b, t, heads * dh), jnp.bfloat16),
        scratch_shapes=[pltpu.VMEM((t, 2 * dh), jnp.bfloat16),
                        pltpu.VMEM((t, 2 * dh), jnp.bfloat16),
                        pltpu.VMEM((t, t), jnp.float32)],
        compiler_params=_cparams(2),
        name="moba_attention",
    )(rel_bias, proj, proj, proj, diag_tiles, near_tiles)


def _scan_rows(x, op, identity):
    n = x.shape[0]
    row = lax.broadcasted_iota(jnp.int32, x.shape, 0)
    s = 1
    while s < n:
        x = op(x, jnp.where(row >= s, pltpu.roll(x, s, axis=0), identity))
        s *= 2
    return x


def _mlstm_gates_kernel(if_ref, bias_ref, g_ref, mb_ref, eb_ref):
    pre = if_ref[0] + bias_ref[...]
    lanes = pre.shape[1]
    i_pre = pre
    f_pre = pltpu.roll(pre, lanes - MLSTM_HEADS, axis=1)
    log_f = -(jnp.maximum(-f_pre, 0.0) + jnp.log1p(jnp.exp(-jnp.abs(f_pre))))
    b = _scan_rows(log_f, jnp.add, 0.0)
    g = i_pre - b
    m_rel = jnp.maximum(_scan_rows(g, jnp.maximum, -jnp.inf), 0.0)
    e = jnp.exp(-(b + m_rel))
    g_t = (g * LOG2E).T
    m_rel = m_rel * LOG2E
    t = pre.shape[0]
    for hh in range(MLSTM_HEADS):
        g_ref[0, hh] = g_t[hh:hh + 1, :]
        mb_ref[0, hh] = jnp.broadcast_to(m_rel[:, hh:hh + 1], (t, lanes))
        eb_ref[0, hh] = jnp.broadcast_to(e[:, hh:hh + 1], (t, lanes))


def _mlstm_gates(if_pre, bias_row):
    b, t, lanes = if_pre.shape
    col = jax.ShapeDtypeStruct((b, MLSTM_HEADS, t, lanes), jnp.float32)
    col_spec = pl.BlockSpec((1, MLSTM_HEADS, t, lanes), lambda bi: (bi, 0, 0, 0))
    return pl.pallas_call(
        _mlstm_gates_kernel,
        grid=(b,),
        in_specs=[pl.BlockSpec((1, t, lanes), lambda bi: (bi, 0, 0)),
                  pl.BlockSpec((1, lanes), lambda bi: (0, 0))],
        out_specs=[pl.BlockSpec((1, MLSTM_HEADS, 1, t), lambda bi: (bi, 0, 0, 0)), col_spec, col_spec],
        out_shape=[jax.ShapeDtypeStruct((b, MLSTM_HEADS, 1, t), jnp.float32), col, col],
        compiler_params=_cparams(1),
        name="mlstm_gates",
    )(if_pre, bias_row)


def _causal_conv_silu(x, w):
    head_rows = 8
    row = lax.broadcasted_iota(jnp.int32, (head_rows, x.shape[1]), 0)
    out = x * w[CONV_WIDTH - 1:CONV_WIDTH, :]
    for back in range(1, CONV_WIDTH):
        rolled = pltpu.roll(x, back, axis=0)
        head = jnp.where(row >= back, rolled[0:head_rows, :], 0.0)
        shifted = jnp.concatenate([head, rolled[head_rows:, :]], axis=0)
        out = out + shifted * w[CONV_WIDTH - 1 - back:CONV_WIDTH - back, :]
    return _silu(out)


def _mlstm_kernel(q_ref, k_ref, v_ref, o_ref, wq_ref, wk_ref, g_ref, mb_ref, eb_ref, gn_ref, y_ref,
                  qc_ref, kc_ref):
    blk = MLSTM_BLOCK
    t, d = q_ref.shape[1], q_ref.shape[2]
    lanes = mb_ref.shape[3]
    qc_ref[...] = _causal_conv_silu(q_ref[0].astype(jnp.float32), wq_ref[...]).astype(jnp.bfloat16)
    kc = _causal_conv_silu(k_ref[0].astype(jnp.float32), wk_ref[...]) * (d ** -0.5)
    kc_ref[...] = kc.astype(jnp.bfloat16)
    qi = lax.broadcasted_iota(jnp.int32, (blk, blk), 0)
    ki = lax.broadcasted_iota(jnp.int32, (blk, blk), 1)

    for i in range(t // blk):
        rows = slice(i * blk, (i + 1) * blk)
        n_keys = (i + 1) * blk
        qk = _dot_nt(qc_ref[rows, :], kc_ref[0:n_keys, :])
        m_col = mb_ref[0, 0, rows, :]
        m_full = jnp.concatenate([m_col] * (n_keys // lanes), axis=1)
        w = jnp.exp2(g_ref[0, 0, :, 0:n_keys] - m_full)
        w_own = jnp.where(ki <= qi, w[:, i * blk:], 0.0)
        w = jnp.concatenate([w[:, :i * blk], w_own], axis=1) if i else w_own
        s = qk * w
        den = jnp.sum(s, axis=-1, keepdims=True)
        num = _dot(s.astype(jnp.bfloat16), v_ref[0, 0:n_keys, :])
        hid = num / jnp.maximum(jnp.abs(den), eb_ref[0, 0, rows, 0:1])
        hn = hid * lax.rsqrt(jnp.mean(hid * hid, axis=-1, keepdims=True) + RMS_EPS) * gn_ref[...]
        y_ref[0, rows, :] = (jax.nn.sigmoid(o_ref[0, rows, :].astype(jnp.float32)) * hn).astype(y_ref.dtype)


def _mlstm(proj, conv_w, g_rows, m_cols, e_cols, g_norm, col0):
    b, t, _ = proj.shape
    heads = MLSTM_HEADS
    d = g_norm.shape[0] // heads
    c0 = col0 // d
    lanes = m_cols.shape[3]
    full = lambda off: pl.BlockSpec((1, t, d), lambda bi, h: (bi, 0, c0 + off * heads + h))
    col_spec = pl.BlockSpec((1, 1, t, lanes), lambda bi, h: (bi, h, 0, 0))
    return pl.pallas_call(
        _mlstm_kernel,
        grid=(b, heads),
        in_specs=[full(0), full(1), full(2), full(3),
                  pl.BlockSpec((CONV_WIDTH, d), lambda bi, h: (0, h)),
                  pl.BlockSpec((CONV_WIDTH, d), lambda bi, h: (0, heads + h)),
                  pl.BlockSpec((1, 1, 1, t), lambda bi, h: (bi, h, 0, 0)),
                  col_spec, col_spec,
                  pl.BlockSpec((1, d), lambda bi, h: (0, h))],
        out_specs=pl.BlockSpec((1, t, d), lambda bi, h: (bi, 0, h)),
        out_shape=jax.ShapeDtypeStruct((b, t, heads * d), jnp.bfloat16),
        scratch_shapes=[pltpu.VMEM((t, d), jnp.bfloat16), pltpu.VMEM((t, d), jnp.bfloat16)],
        compiler_params=_cparams(2),
        name="mlstm",
    )(proj, proj, proj, proj, conv_w, conv_w, g_rows, m_cols, e_cols, g_norm.reshape(1, -1))


def _sgu_kernel(u_ref, v_ref, w_ref, bt_ref, g_ref, y_ref):
    chunk = SGU_CHUNK
    tt, width = u_ref.shape[1], u_ref.shape[2]
    gd = width // SGU_GROUPS
    v = _gelu_tanh(v_ref[0].astype(jnp.float32))
    vc = v - jnp.mean(v, axis=-1, keepdims=True)
    var = jnp.mean(vc * vc, axis=-1, keepdims=True)
    vn = (vc * lax.rsqrt(var + LN_EPS) * g_ref[...]).astype(jnp.bfloat16)
    row = lax.broadcasted_iota(jnp.int32, (chunk, chunk), 0)
    col = lax.broadcasted_iota(jnp.int32, (chunk, chunk), 1)
    for g in range(SGU_GROUPS):
        w = jnp.where(col <= row, w_ref[g], 0.0).astype(jnp.bfloat16)
        bias = bt_ref[:, g:g + 1]
        cols = slice(g * gd, (g + 1) * gd)
        for n in range(tt // chunk):
            rows = slice(n * chunk, (n + 1) * chunk)
            mixed = _dot(w, vn[rows, cols]) + bias
            u = _gelu_tanh(u_ref[0, rows, cols].astype(jnp.float32))
            y_ref[0, rows, cols] = (u * mixed).astype(y_ref.dtype)


def _sgu(proj, w_s, b_s, g_v, col0):
    b, t, _ = proj.shape
    width = g_v.shape[0]
    tt = 512
    c0 = col0 // width
    return pl.pallas_call(
        _sgu_kernel,
        grid=(b, t // tt),
        in_specs=[pl.BlockSpec((1, tt, width), lambda bi, ti: (bi, ti, c0)),
                  pl.BlockSpec((1, tt, width), lambda bi, ti: (bi, ti, c0 + 1)),
                  pl.BlockSpec(w_s.shape, lambda bi, ti: (0, 0, 0)),
                  pl.BlockSpec((SGU_CHUNK, SGU_GROUPS), lambda bi, ti: (0, 0)),
                  pl.BlockSpec((1, width), lambda bi, ti: (0, 0))],
        out_specs=pl.BlockSpec((1, tt, width), lambda bi, ti: (bi, ti, 0)),
        out_shape=jax.ShapeDtypeStruct((b, t, width), jnp.bfloat16),
        compiler_params=_cparams(2),
        name="sgu",
    )(proj, proj, w_s, b_s.T, g_v.reshape(1, width))


def kernel(x, c, w_ada, b_ada, ada_table, rel_bias, g_mix_pre, g_mix_post, g_ffn_pre, g_ffn_post,
           w_in, conv_qk, b_if, g_mlstm, w_sgu, b_sgu, g_sgu, w_out, w_gate, w_up, w_down):
    bsz, t, d = x.shape
    depth = w_in.shape[0]
    bf16 = jnp.bfloat16
    att_heads = rel_bias.shape[1]
    att_w = att_heads * ATT_HEAD_DIM
    ml_w = g_mlstm.shape[1]
    sgu_w = g_sgu.shape[1]
    n_gate = 2 * MLSTM_HEADS
    gate0 = 3 * att_w + 4 * ml_w
    ffn = w_gate.shape[2]
    m = bsz * t

    base_mod = _ada_base(c, w_ada, b_ada)
    diag_tiles, near_tiles = _rel_bias_tiles(rel_bias)
    w_in_t = jnp.swapaxes(w_in, 1, 2)

    h = _prenorm(x, g_mix_pre[0], base_mod, ada_table[0], 0, 1)
    for l in range(depth):
        tab = ada_table[l]
        w_if_t = jnp.pad(w_in_t[l:l + 1, gate0:gate0 + n_gate, :], ((0, 0), (0, 128 - n_gate), (0, 0)))
        w_uv_t = w_in_t[l:l + 1, gate0 + n_gate:, :]
        bias_row = jnp.pad(b_if[l].reshape(1, n_gate), ((0, 0), (0, 128 - n_gate)))

        h = h.reshape(m, d)
        proj = _matmul_stream_nt(h, w_in_t, l, gate0, 1024, 1024, bf16, "in_proj").reshape(bsz, t, gate0)
        proj_uv = _matmul_stream_nt(h, w_uv_t, 0, 2 * sgu_w, 1024, 1024, bf16, "in_proj_uv").reshape(bsz, t, 2 * sgu_w)
        if_pre = _matmul_wcast_nt(h, w_if_t, 0, 128, 1024, 128, jnp.float32, "gate_proj").reshape(bsz, t, 128)

        y_att = _moba_attention(proj, rel_bias, diag_tiles, near_tiles, att_heads)
        g_rows, m_cols, e_cols = _mlstm_gates(if_pre, bias_row)
        y_ml = _mlstm(proj, conv_qk[l], g_rows, m_cols, e_cols, g_mlstm[l], 3 * att_w)
        y_sgu = _sgu(proj_uv, w_sgu[l], b_sgu[l], g_sgu[l], 0)
        y = _matmul3_stream(y_att.reshape(m, att_w), y_ml.reshape(m, ml_w), y_sgu.reshape(m, sgu_w),
                            w_out, l, 1024, 1024, bf16, "out_proj")
        x, h = _postnorm_prenorm(x, y.reshape(bsz, t, d), g_mix_post[l], base_mod, tab, 2,
                                 g_ffn_pre[l], tab, 3, 4)

        act, w_down_b = _swiglu_matmul(h.reshape(m, d), w_gate, w_up, w_down, l, 2048, 256)
        y = _matmul_rows(act, w_down_b, 512, 512, bf16, "ffn_down").reshape(bsz, t, d)
        if l + 1 < depth:
            x, h = _postnorm_prenorm(x, y, g_ffn_post[l], base_mod, tab, 5,
                                     g_mix_pre[l + 1], ada_table[l + 1], 0, 1)
        else:
            x = _postnorm_residual(x, y, g_ffn_post[l], base_mod, tab, 5)
    return x
```

```python
import functools
import math

import numpy as np
import jax
import jax.numpy as jnp
from jax import lax
from jax.experimental import pallas as pl
from jax.experimental.pallas import tpu as pltpu

ATT_HEAD_DIM = 128
MOBA_BLOCK = 256
MOBA_TOPK = 3
REL_BUCKETS = 32
REL_MAX_DIST = 128
MLSTM_HEADS = 4
MLSTM_BLOCK = 256
CONV_WIDTH = 4
SGU_GROUPS = 8
SGU_CHUNK = 128
N_MOD = 6
RMS_EPS = 1e-6
LN_EPS = 1e-5
MASKED = -1e30
LOG2E = math.log2(math.e)

VMEM_LIMIT_BYTES = 56 * 1024 * 1024


def _cparams(n_axes):
    return pltpu.CompilerParams(dimension_semantics=("arbitrary",) * n_axes,
                                vmem_limit_bytes=VMEM_LIMIT_BYTES)


def _dot(a, b):
    return jnp.dot(a, b, preferred_element_type=jnp.float32)


def _dot_nt(a, b):
    return lax.dot_general(a, b, (((1,), (1,)), ((), ())), preferred_element_type=jnp.float32)


def _silu(x):
    return x * jax.nn.sigmoid(x)


def _gelu_tanh(x):
    return 0.5 * x * (1.0 + jnp.tanh(math.sqrt(2.0 / math.pi) * (x + 0.044715 * (x * x * x))))


def _ada_kernel(c_ref, w_ref, b_ref, o_ref):
    a = _silu(c_ref[...]).astype(jnp.bfloat16)
    o_ref[...] = _dot(a, w_ref[...].astype(jnp.bfloat16)) + b_ref[...]


def _ada_base(c, w_ada, b_ada):
    b, d = c.shape
    n = w_ada.shape[1]
    rows = 16
    c_pad = jnp.zeros((rows, d), jnp.float32).at[:b].set(c)
    tn = 1024
    out = pl.pallas_call(
        _ada_kernel,
        grid=(n // tn,),
        in_specs=[pl.BlockSpec((rows, d), lambda j: (0, 0)),
                  pl.BlockSpec((d, tn), lambda j: (0, j)),
                  pl.BlockSpec((1, tn), lambda j: (0, j))],
        out_specs=pl.BlockSpec((rows, tn), lambda j: (0, j)),
        out_shape=jax.ShapeDtypeStruct((rows, n), jnp.float32),
        compiler_params=_cparams(1),
        name="ada_base",
    )(c_pad, w_ada, b_ada.reshape(1, n))
    return out[:b].reshape(b, N_MOD, n // N_MOD)


def _prenorm_kernel(shift_idx, scale_idx, x_ref, g_ref, mod_ref, tab_ref, o_ref):
    x = x_ref[0]
    y = x * lax.rsqrt(jnp.mean(x * x, axis=-1, keepdims=True) + RMS_EPS) * g_ref[...]
    scale = mod_ref[0, scale_idx:scale_idx + 1, :] + tab_ref[scale_idx:scale_idx + 1, :]
    shift = mod_ref[0, shift_idx:shift_idx + 1, :] + tab_ref[shift_idx:shift_idx + 1, :]
    o_ref[0] = (y * (1.0 + scale) + shift).astype(o_ref.dtype)


def _prenorm(x, g, base_mod, tab, shift_idx, scale_idx):
    b, t, d = x.shape
    tt = 256
    return pl.pallas_call(
        functools.partial(_prenorm_kernel, shift_idx, scale_idx),
        grid=(b, t // tt),
        in_specs=[pl.BlockSpec((1, tt, d), lambda bi, ti: (bi, ti, 0)),
                  pl.BlockSpec((1, d), lambda bi, ti: (0, 0)),
                  pl.BlockSpec((1, N_MOD, d), lambda bi, ti: (bi, 0, 0)),
                  pl.BlockSpec((N_MOD, d), lambda bi, ti: (0, 0))],
        out_specs=pl.BlockSpec((1, tt, d), lambda bi, ti: (bi, ti, 0)),
        out_shape=jax.ShapeDtypeStruct((b, t, d), jnp.bfloat16),
        compiler_params=_cparams(2),
        name="prenorm",
    )(x, g.reshape(1, d), base_mod, tab)


def _postnorm_kernel(gate_idx, x_ref, y_ref, g_ref, mod_ref, tab_ref, o_ref):
    y = y_ref[0].astype(jnp.float32)
    yn = y * lax.rsqrt(jnp.mean(y * y, axis=-1, keepdims=True) + RMS_EPS) * g_ref[...]
    gate = mod_ref[0, gate_idx:gate_idx + 1, :] + tab_ref[gate_idx:gate_idx + 1, :]
    o_ref[0] = x_ref[0] + gate * yn


def _post_pre_kernel(gate_idx, shift_idx, scale_idx, x_ref, y_ref, g_ref, gn_ref, mod_ref, tab_ref, tabn_ref,
                     o_ref, h_ref):
    _postnorm_kernel(gate_idx, x_ref, y_ref, g_ref, mod_ref, tab_ref, o_ref)
    _prenorm_kernel(shift_idx, scale_idx, o_ref, gn_ref, mod_ref, tabn_ref, h_ref)


def _postnorm_prenorm(x, y, g_post, base_mod, tab, gate_idx, g_pre, tab_next, shift_idx, scale_idx):
    b, t, d = x.shape
    tt = 256
    row_spec = pl.BlockSpec((1, tt, d), lambda bi, ti: (bi, ti, 0))
    vec_spec = pl.BlockSpec((1, d), lambda bi, ti: (0, 0))
    tab_spec = pl.BlockSpec((N_MOD, d), lambda bi, ti: (0, 0))
    return pl.pallas_call(
        functools.partial(_post_pre_kernel, gate_idx, shift_idx, scale_idx),
        grid=(b, t // tt),
        in_specs=[row_spec, row_spec, vec_spec, vec_spec,
                  pl.BlockSpec((1, N_MOD, d), lambda bi, ti: (bi, 0, 0)), tab_spec, tab_spec],
        out_specs=[row_spec, row_spec],
        out_shape=[jax.ShapeDtypeStruct((b, t, d), jnp.float32), jax.ShapeDtypeStruct((b, t, d), jnp.bfloat16)],
        compiler_params=_cparams(2),
        name="postnorm_prenorm",
    )(x, y, g_post.reshape(1, d), g_pre.reshape(1, d), base_mod, tab, tab_next)


def _postnorm_residual(x, y, g, base_mod, tab, gate_idx):
    b, t, d = x.shape
    tt = 256
    return pl.pallas_call(
        functools.partial(_postnorm_kernel, gate_idx),
        grid=(b, t // tt),
        in_specs=[pl.BlockSpec((1, tt, d), lambda bi, ti: (bi, ti, 0)),
                  pl.BlockSpec((1, tt, d), lambda bi, ti: (bi, ti, 0)),
                  pl.BlockSpec((1, d), lambda bi, ti: (0, 0)),
                  pl.BlockSpec((1, N_MOD, d), lambda bi, ti: (bi, 0, 0)),
                  pl.BlockSpec((N_MOD, d), lambda bi, ti: (0, 0))],
        out_specs=pl.BlockSpec((1, tt, d), lambda bi, ti: (bi, ti, 0)),
        out_shape=jax.ShapeDtypeStruct((b, t, d), jnp.float32),
        compiler_params=_cparams(2),
        name="postnorm_residual",
    )(x, y, g.reshape(1, d), base_mod, tab)


def _streamed_steps(w_refs, wb_refs, multiply):
    t, i = pl.program_id(0), pl.program_id(1)

    def stage(slot):
        for w_ref, wb_ref in zip(w_refs, wb_refs):
            rows = w_ref.shape[1]
            wb_ref[slot, pl.ds(pl.multiple_of(i * rows, rows), rows), :] = w_ref[0].astype(jnp.bfloat16)

    @pl.when(t == 0)
    def _():
        stage(0)

    for parity in (0, 1):
        @pl.when((t > 0) & (t % 2 == parity))
        def _():
            stage(parity)
            multiply(*[wb_ref[1 - parity] for wb_ref in wb_refs])


def _stream_specs(tm, n_tiles, x_widths, out_tn):
    row = lambda t, i: jnp.where(t > 0, i, 0)
    x_specs = [pl.BlockSpec((tm, kx), lambda t, i: (row(t, i), 0)) for kx in x_widths]
    out_spec = pl.BlockSpec((tm, out_tn), lambda t, i: (row(t, i), jnp.maximum(t - 1, 0)))
    return x_specs, out_spec


def _mm_stream_nt_kernel(x_ref, w_ref, o_ref, wb_ref):
    def multiply(w):
        o_ref[...] = _dot_nt(x_ref[...], w).astype(o_ref.dtype)

    _streamed_steps([w_ref], [wb_ref], multiply)


def _matmul_stream_nt(x, wt_stack, layer, n_cols, tm, tn, out_dtype, name):
    m, k = x.shape
    n_tiles, n_rows = n_cols // tn, m // tm
    piece = tn // n_rows
    x_specs, out_spec = _stream_specs(tm, n_tiles, [k], tn)
    return pl.pallas_call(
        _mm_stream_nt_kernel,
        grid=(n_tiles + 1, n_rows),
        in_specs=x_specs + [pl.BlockSpec(
            (1, piece, k), lambda t, i: (layer, jnp.minimum(t, n_tiles - 1) * n_rows + i, 0))],
        out_specs=out_spec,
        out_shape=jax.ShapeDtypeStruct((m, n_cols), out_dtype),
        scratch_shapes=[pltpu.VMEM((2, tn, k), jnp.bfloat16)],
        compiler_params=_cparams(2),
        name=name,
    )(x, wt_stack)


def _stream_weight_spec(w_stack, layer, tn, n_rows):
    k, n = w_stack.shape[1], w_stack.shape[2]
    n_tiles = n // tn
    return pl.BlockSpec((1, k // n_rows, tn), lambda t, i: (layer, i, jnp.minimum(t, n_tiles - 1)))


def _mm3_stream_kernel(xa_ref, xb_ref, xc_ref, w_ref, o_ref, wb_ref):
    def multiply(w):
        ka, kb = xa_ref.shape[1], xb_ref.shape[1]
        o_ref[...] = (_dot(xa_ref[...], w[0:ka, :]) + _dot(xb_ref[...], w[ka:ka + kb, :])
                      + _dot(xc_ref[...], w[ka + kb:, :])).astype(o_ref.dtype)

    _streamed_steps([w_ref], [wb_ref], multiply)


def _matmul3_stream(xa, xb, xc, w_stack, layer, tm, tn, out_dtype, name):
    m = xa.shape[0]
    k, n = w_stack.shape[1], w_stack.shape[2]
    n_tiles, n_rows = n // tn, m // tm
    x_specs, out_spec = _stream_specs(tm, n_tiles, [xa.shape[1], xb.shape[1], xc.shape[1]], tn)
    return pl.pallas_call(
        _mm3_stream_kernel,
        grid=(n_tiles + 1, n_rows),
        in_specs=x_specs + [_stream_weight_spec(w_stack, layer, tn, n_rows)],
        out_specs=out_spec,
        out_shape=jax.ShapeDtypeStruct((m, n), out_dtype),
        scratch_shapes=[pltpu.VMEM((2, k, tn), jnp.bfloat16)],
        compiler_params=_cparams(2),
        name=name,
    )(xa, xb, xc, w_stack)


def _swiglu_stream_kernel(x_ref, wg_ref, wu_ref, wd_ref, o_ref, wdb_ref, wgb_ref, wub_ref):
    def multiply(wg, wu):
        x = x_ref[...]
        o_ref[...] = (_silu(_dot(x, wg)) * _dot(x, wu)).astype(o_ref.dtype)
        wdb_ref[...] = wd_ref[0].astype(jnp.bfloat16)

    _streamed_steps([wg_ref, wu_ref], [wgb_ref, wub_ref], multiply)


def _swiglu_matmul(x, wg_stack, wu_stack, wd_stack, layer, tm, tn):
    m, k = x.shape
    n = wg_stack.shape[2]
    n_tiles, n_rows = n // tn, m // tm
    slab = wd_stack.shape[1] // (n_tiles * n_rows)
    assert slab * n_tiles * n_rows == wd_stack.shape[1]
    slab_index = lambda t, i: jnp.maximum((t - 1) * n_rows + i, 0)
    x_specs, out_spec = _stream_specs(tm, n_tiles, [k], tn)
    w_spec = _stream_weight_spec(wg_stack, layer, tn, n_rows)
    return pl.pallas_call(
        _swiglu_stream_kernel,
        grid=(n_tiles + 1, n_rows),
        in_specs=x_specs + [w_spec, w_spec,
                            pl.BlockSpec((1, slab, wd_stack.shape[2]), lambda t, i: (layer, slab_index(t, i), 0))],
        out_specs=[out_spec, pl.BlockSpec((slab, wd_stack.shape[2]), lambda t, i: (slab_index(t, i), 0))],
        out_shape=[jax.ShapeDtypeStruct((m, n), jnp.bfloat16),
                   jax.ShapeDtypeStruct(wd_stack.shape[1:], jnp.bfloat16)],
        scratch_shapes=[pltpu.VMEM((2, k, tn), jnp.bfloat16), pltpu.VMEM((2, k, tn), jnp.bfloat16)],
        compiler_params=_cparams(2),
        name="ffn_gate_up",
    )(x, wg_stack, wu_stack, wd_stack)


def _mm_rows_kernel(x_ref, w_ref, o_ref):
    o_ref[...] = _dot(x_ref[...], w_ref[...]).astype(o_ref.dtype)


def _matmul_rows(x, w, tm, tn, out_dtype, name):
    m, k = x.shape
    n = w.shape[1]
    return pl.pallas_call(
        _mm_rows_kernel,
        grid=(m // tm, n // tn),
        in_specs=[pl.BlockSpec((tm, k), lambda i, j: (i, 0)),
                  pl.BlockSpec((k, tn), lambda i, j: (0, j))],
        out_specs=pl.BlockSpec((tm, tn), lambda i, j: (i, j)),
        out_shape=jax.ShapeDtypeStruct((m, n), out_dtype),
        compiler_params=_cparams(2),
        name=name,
    )(x, w)


def _rel_bucket_starts():
    max_exact = REL_BUCKETS // 2
    d = np.arange(0, 4 * REL_MAX_DIST)
    ratio = np.maximum(d, 1).astype(np.float32) / np.float32(max_exact)
    large = max_exact + (np.log(ratio) / np.float32(math.log(REL_MAX_DIST / max_exact))
                         * np.float32(REL_BUCKETS - max_exact)).astype(np.int32)
    bucket = np.where(d < max_exact, d, np.minimum(large, REL_BUCKETS - 1))
    assert np.all(np.diff(bucket) >= 0)
    return [int(np.argmax(bucket == j)) for j in range(REL_BUCKETS)]


def _rel_tiles_kernel(starts, rel_ref, diag_ref, near_ref):
    h = pl.program_id(0)
    qry = lax.broadcasted_iota(jnp.int32, (MOBA_BLOCK, MOBA_BLOCK), 0)
    key = lax.broadcasted_iota(jnp.int32, (MOBA_BLOCK, MOBA_BLOCK), 1)
    dist = qry - key

    def bias_of(d):
        val = jnp.full(d.shape, rel_ref[0, h], jnp.float32)
        for j in range(1, REL_BUCKETS):
            val = jnp.where(d >= starts[j], rel_ref[j, h], val)
        return val

    diag_ref[0] = jnp.where(dist >= 0, bias_of(dist) * LOG2E, MASKED)
    near_ref[0] = bias_of(dist + MOBA_BLOCK) * LOG2E


def _rel_bias_tiles(rel_bias):
    heads = rel_bias.shape[1]
    tile = jax.ShapeDtypeStruct((heads, MOBA_BLOCK, MOBA_BLOCK), jnp.float32)
    spec = pl.BlockSpec((1, MOBA_BLOCK, MOBA_BLOCK), lambda h: (h, 0, 0))
    return pl.pallas_call(
        functools.partial(_rel_tiles_kernel, _rel_bucket_starts()),
        grid=(heads,),
        in_specs=[pl.BlockSpec(memory_space=pltpu.SMEM)],
        out_specs=[spec, spec],
        out_shape=[tile, tile],
        compiler_params=_cparams(1),
        name="rel_bias_tiles",
    )(rel_bias)


def _moba_kernel(n_blocks, rel_ref, q_ref, k_ref, v_ref, diag_ref, near_ref, o_ref,
                 kaug_ref, qaug_ref, s_ref):
    h = pl.program_id(1)
    blk, dh = MOBA_BLOCK, ATT_HEAD_DIM
    t = n_blocks * blk
    scale = dh ** -0.5 * LOG2E
    far_bias = rel_ref[REL_BUCKETS - 1, h] * LOG2E
    k = k_ref[0]

    row = lax.broadcasted_iota(jnp.int32, (t, dh), 0)
    col = lax.broadcasted_iota(jnp.int32, (t, dh), 1)
    lo = col * blk
    kaug_ref[:, 0:dh] = k
    kaug_ref[:, dh:2 * dh] = jnp.where((row >= lo) & (row < lo + blk), 1.0, 0.0).astype(jnp.bfloat16)

    brow = lax.broadcasted_iota(jnp.int32, (n_blocks, t), 0) * blk
    bcol = lax.broadcasted_iota(jnp.int32, (n_blocks, t), 1)
    avg = jnp.where((bcol >= brow) & (bcol < brow + blk), 1.0 / blk, 0.0).astype(jnp.bfloat16)
    kmean = _dot(avg, k)
    km_hi = kmean.astype(jnp.bfloat16)
    km_lo = (kmean - km_hi.astype(jnp.float32)).astype(jnp.bfloat16)

    q = q_ref[0]
    s_gate = _dot_nt(km_hi, q) + _dot_nt(km_lo, q)
    bid = lax.broadcasted_iota(jnp.int32, s_gate.shape, 0)
    past = (bid + 1) * blk <= lax.broadcasted_iota(jnp.int32, s_gate.shape, 1)
    s_gate = jnp.where(past, s_gate, -jnp.inf)
    rank = jnp.zeros(s_gate.shape, jnp.int32)
    for j in range(n_blocks):
        s_j = s_gate[j:j + 1, :]
        beats = (s_j > s_gate) | ((s_j == s_gate) & (j < bid))
        rank = rank + jnp.where(beats, 1, 0)
    offs = jnp.where(past & (rank >= MOBA_TOPK), MASKED, 0.0)
    offs = jnp.concatenate([offs, jnp.zeros((dh - n_blocks, t), jnp.float32)], axis=0)
    qaug_ref[:, 0:dh] = q
    qaug_ref[:, dh:2 * dh] = offs.T.astype(jnp.bfloat16)

    def logits_of_key_block(j):
        cols = slice(j * blk, (j + 1) * blk)
        s_ref[j * blk:, cols] = _dot_nt(qaug_ref[j * blk:, :], kaug_ref[cols, :])

    def attend_query_block(i):
        rows = slice(i * blk, (i + 1) * blk)
        n_keys = (i + 1) * blk
        s = s_ref[rows, 0:n_keys] * scale
        parts = []
        if i >= 2:
            parts.append(s[:, :(i - 1) * blk] + far_bias)
        if i >= 1:
            parts.append(s[:, (i - 1) * blk:i * blk] + near_ref[0])
        parts.append(s[:, i * blk:] + diag_ref[0])
        s = jnp.concatenate(parts, axis=1) if len(parts) > 1 else parts[0]
        p = jnp.exp2(s - jnp.max(s, axis=1, keepdims=True))
        denom = jnp.sum(p, axis=1, keepdims=True)
        out = _dot(p.astype(jnp.bfloat16), v_ref[0, 0:n_keys, :]) / denom
        o_ref[0, rows, :] = out.astype(o_ref.dtype)

    logits_of_key_block(0)
    for i in range(n_blocks):
        if i + 1 < n_blocks:
            logits_of_key_block(i + 1)
        attend_query_block(i)


def _moba_attention(proj, rel_bias, diag_tiles, near_tiles, heads):
    b, t, _ = proj.shape
    blk, dh = MOBA_BLOCK, ATT_HEAD_DIM
    n_blocks = t // blk
    assert n_blocks <= dh
    tile_spec = pl.BlockSpec((1, blk, blk), lambda bi, h: (h, 0, 0))
    seq_spec = lambda off: pl.BlockSpec((1, t, dh), lambda bi, h: (bi, 0, off * heads + h))
    return pl.pallas_call(
        functools.partial(_moba_kernel, n_blocks),
        grid=(b, heads),
        in_specs=[pl.BlockSpec(memory_space=pltpu.SMEM), seq_spec(0), seq_spec(1), seq_spec(2),
                  tile_spec, tile_spec],
        out_specs=pl.BlockSpec((1, t, dh), lambda bi, h: (bi, 0, h)),
        out_shape=jax.ShapeDtypeStruct((b, t, heads * dh), jnp.bfloat16),
        scratch_shapes=[pltpu.VMEM((t, 2 * dh), jnp.bfloat16),
                        pltpu.VMEM((t, 2 * dh), jnp.bfloat16),
                        pltpu.VMEM((t, t), jnp.float32)],
        compiler_params=_cparams(2),
        name="moba_attention",
    )(rel_bias, proj, proj, proj, diag_tiles, near_tiles)


def _scan_rows(x, op, identity):
    n = x.shape[0]
    row = lax.broadcasted_iota(jnp.int32, x.shape, 0)
    s = 1
    while s < n:
        x = op(x, jnp.where(row >= s, pltpu.roll(x, s, axis=0), identity))
        s *= 2
    return x


def _mlstm_gates_kernel(h_ref, w_ref, bias_ref, g_ref, mb_ref, eb_ref):
    pre = _dot_nt(h_ref[0], w_ref[0].astype(jnp.bfloat16)) + bias_ref[...]
    lanes = pre.shape[1]
    i_pre = pre
    f_pre = pltpu.roll(pre, lanes - MLSTM_HEADS, axis=1)
    log_f = -(jnp.maximum(-f_pre, 0.0) + jnp.log1p(jnp.exp(-jnp.abs(f_pre))))
    b = _scan_rows(log_f, jnp.add, 0.0)
    g = i_pre - b
    m_rel = jnp.maximum(_scan_rows(g, jnp.maximum, -jnp.inf), 0.0)
    e = jnp.exp(-(b + m_rel))
    g_t = (g * LOG2E).T
    m_rel = m_rel * LOG2E
    t = pre.shape[0]
    for hh in range(MLSTM_HEADS):
        g_ref[0, hh] = g_t[hh:hh + 1, :]
        mb_ref[0, hh] = jnp.broadcast_to(m_rel[:, hh:hh + 1], (t, lanes))
        eb_ref[0, hh] = jnp.broadcast_to(e[:, hh:hh + 1], (t, lanes))


def _mlstm_gates(h, w_if_t, bias_row):
    b, t, d = h.shape
    lanes = w_if_t.shape[1]
    col = jax.ShapeDtypeStruct((b, MLSTM_HEADS, t, lanes), jnp.float32)
    col_spec = pl.BlockSpec((1, MLSTM_HEADS, t, lanes), lambda bi: (bi, 0, 0, 0))
    return pl.pallas_call(
        _mlstm_gates_kernel,
        grid=(b,),
        in_specs=[pl.BlockSpec((1, t, d), lambda bi: (bi, 0, 0)),
                  pl.BlockSpec((1, lanes, d), lambda bi: (0, 0, 0)),
                  pl.BlockSpec((1, lanes), lambda bi: (0, 0))],
        out_specs=[pl.BlockSpec((1, MLSTM_HEADS, 1, t), lambda bi: (bi, 0, 0, 0)), col_spec, col_spec],
        out_shape=[jax.ShapeDtypeStruct((b, MLSTM_HEADS, 1, t), jnp.float32), col, col],
        compiler_params=_cparams(1),
        name="mlstm_gates",
    )(h, w_if_t, bias_row)


def _causal_conv_silu(x, w):
    head_rows = 8
    row = lax.broadcasted_iota(jnp.int32, (head_rows, x.shape[1]), 0)
    out = x * w[CONV_WIDTH - 1:CONV_WIDTH, :]
    for back in range(1, CONV_WIDTH):
        rolled = pltpu.roll(x, back, axis=0)
        head = jnp.where(row >= back, rolled[0:head_rows, :], 0.0)
        shifted = jnp.concatenate([head, rolled[head_rows:, :]], axis=0)
        out = out + shifted * w[CONV_WIDTH - 1 - back:CONV_WIDTH - back, :]
    return _silu(out)


def _mlstm_kernel(q_ref, k_ref, v_ref, o_ref, wq_ref, wk_ref, g_ref, mb_ref, eb_ref, gn_ref, y_ref,
                  qc_ref, kc_ref):
    blk = MLSTM_BLOCK
    t, d = q_ref.shape[1], q_ref.shape[2]
    lanes = mb_ref.shape[3]
    qc_ref[...] = _causal_conv_silu(q_ref[0].astype(jnp.float32), wq_ref[...]).astype(jnp.bfloat16)
    kc = _causal_conv_silu(k_ref[0].astype(jnp.float32), wk_ref[...]) * (d ** -0.5)
    kc_ref[...] = kc.astype(jnp.bfloat16)
    qi = lax.broadcasted_iota(jnp.int32, (blk, blk), 0)
    ki = lax.broadcasted_iota(jnp.int32, (blk, blk), 1)

    for i in range(t // blk):
        rows = slice(i * blk, (i + 1) * blk)
        n_keys = (i + 1) * blk
        qk = _dot_nt(qc_ref[rows, :], kc_ref[0:n_keys, :])
        m_col = mb_ref[0, 0, rows, :]
        m_full = jnp.concatenate([m_col] * (n_keys // lanes), axis=1)
        w = jnp.exp2(g_ref[0, 0, :, 0:n_keys] - m_full)
        w_own = jnp.where(ki <= qi, w[:, i * blk:], 0.0)
        w = jnp.concatenate([w[:, :i * blk], w_own], axis=1) if i else w_own
        s = qk * w
        den = jnp.sum(s, axis=-1, keepdims=True)
        num = _dot(s.astype(jnp.bfloat16), v_ref[0, 0:n_keys, :])
        hid = num / jnp.maximum(jnp.abs(den), eb_ref[0, 0, rows, 0:1])
        hn = hid * lax.rsqrt(jnp.mean(hid * hid, axis=-1, keepdims=True) + RMS_EPS) * gn_ref[...]
        y_ref[0, rows, :] = (jax.nn.sigmoid(o_ref[0, rows, :].astype(jnp.float32)) * hn).astype(y_ref.dtype)


def _mlstm(proj, conv_w, g_rows, m_cols, e_cols, g_norm, col0):
    b, t, _ = proj.shape
    heads = MLSTM_HEADS
    d = g_norm.shape[0] // heads
    c0 = col0 // d
    lanes = m_cols.shape[3]
    full = lambda off: pl.BlockSpec((1, t, d), lambda bi, h: (bi, 0, c0 + off * heads + h))
    col_spec = pl.BlockSpec((1, 1, t, lanes), lambda bi, h: (bi, h, 0, 0))
    return pl.pallas_call(
        _mlstm_kernel,
        grid=(b, heads),
        in_specs=[full(0), full(1), full(2), full(3),
                  pl.BlockSpec((CONV_WIDTH, d), lambda bi, h: (0, h)),
                  pl.BlockSpec((CONV_WIDTH, d), lambda bi, h: (0, heads + h)),
                  pl.BlockSpec((1, 1, 1, t), lambda bi, h: (bi, h, 0, 0)),
                  col_spec, col_spec,
                  pl.BlockSpec((1, d), lambda bi, h: (0, h))],
        out_specs=pl.BlockSpec((1, t, d), lambda bi, h: (bi, 0, h)),
        out_shape=jax.ShapeDtypeStruct((b, t, heads * d), jnp.bfloat16),
        scratch_shapes=[pltpu.VMEM((t, d), jnp.bfloat16), pltpu.VMEM((t, d), jnp.bfloat16)],
        compiler_params=_cparams(2),
        name="mlstm",
    )(proj, proj, proj, proj, conv_w, conv_w, g_rows, m_cols, e_cols, g_norm.reshape(1, -1))


def _sgu_kernel(u_ref, v_ref, w_ref, bt_ref, g_ref, y_ref):
    chunk = SGU_CHUNK
    tt, width = u_ref.shape[1], u_ref.shape[2]
    gd = width // SGU_GROUPS
    v = _gelu_tanh(v_ref[0].astype(jnp.float32))
    vc = v - jnp.mean(v, axis=-1, keepdims=True)
    var = jnp.mean(vc * vc, axis=-1, keepdims=True)
    vn = (vc * lax.rsqrt(var + LN_EPS) * g_ref[...]).astype(jnp.bfloat16)
    row = lax.broadcasted_iota(jnp.int32, (chunk, chunk), 0)
    col = lax.broadcasted_iota(jnp.int32, (chunk, chunk), 1)
    for g in range(SGU_GROUPS):
        w = jnp.where(col <= row, w_ref[g], 0.0).astype(jnp.bfloat16)
        bias = bt_ref[:, g:g + 1]
        cols = slice(g * gd, (g + 1) * gd)
        for n in range(tt // chunk):
            rows = slice(n * chunk, (n + 1) * chunk)
            mixed = _dot(w, vn[rows, cols]) + bias
            u = _gelu_tanh(u_ref[0, rows, cols].astype(jnp.float32))
            y_ref[0, rows, cols] = (u * mixed).astype(y_ref.dtype)


def _sgu(proj, w_s, b_s, g_v, col0):
    b, t, _ = proj.shape
    width = g_v.shape[0]
    tt = 512
    c0 = col0 // width
    return pl.pallas_call(
        _sgu_kernel,
        grid=(b, t // tt),
        in_specs=[pl.BlockSpec((1, tt, width), lambda bi, ti: (bi, ti, c0)),
                  pl.BlockSpec((1, tt, width), lambda bi, ti: (bi, ti, c0 + 1)),
                  pl.BlockSpec(w_s.shape, lambda bi, ti: (0, 0, 0)),
                  pl.BlockSpec((SGU_CHUNK, SGU_GROUPS), lambda bi, ti: (0, 0)),
                  pl.BlockSpec((1, width), lambda bi, ti: (0, 0))],
        out_specs=pl.BlockSpec((1, tt, width), lambda bi, ti: (bi, ti, 0)),
        out_shape=jax.ShapeDtypeStruct((b, t, width), jnp.bfloat16),
        compiler_params=_cparams(2),
        name="sgu",
    )(proj, proj, w_s, b_s.T, g_v.reshape(1, width))


def kernel(x, c, w_ada, b_ada, ada_table, rel_bias, g_mix_pre, g_mix_post, g_ffn_pre, g_ffn_post,
           w_in, conv_qk, b_if, g_mlstm, w_sgu, b_sgu, g_sgu, w_out, w_gate, w_up, w_down):
    bsz, t, d = x.shape
    depth = w_in.shape[0]
    bf16 = jnp.bfloat16
    att_heads = rel_bias.shape[1]
    att_w = att_heads * ATT_HEAD_DIM
    ml_w = g_mlstm.shape[1]
    sgu_w = g_sgu.shape[1]
    n_gate = 2 * MLSTM_HEADS
    gate0 = 3 * att_w + 4 * ml_w
    ffn = w_gate.shape[2]
    m = bsz * t

    base_mod = _ada_base(c, w_ada, b_ada)
    diag_tiles, near_tiles = _rel_bias_tiles(rel_bias)
    w_in_t = jnp.swapaxes(w_in, 1, 2)

    h = _prenorm(x, g_mix_pre[0], base_mod, ada_table[0], 0, 1)
    for l in range(depth):
        tab = ada_table[l]
        w_if_t = jnp.pad(w_in_t[l:l + 1, gate0:gate0 + n_gate, :], ((0, 0), (0, 128 - n_gate), (0, 0)))
        w_uv_t = w_in_t[l:l + 1, gate0 + n_gate:, :]
        bias_row = jnp.pad(b_if[l].reshape(1, n_gate), ((0, 0), (0, 128 - n_gate)))

        h = h.reshape(m, d)
        proj = _matmul_stream_nt(h, w_in_t, l, gate0, 1024, 1024, bf16, "in_proj").reshape(bsz, t, gate0)
        proj_uv = _matmul_stream_nt(h, w_uv_t, 0, 2 * sgu_w, 1024, 1024, bf16, "in_proj_uv").reshape(bsz, t, 2 * sgu_w)

        y_att = _moba_attention(proj, rel_bias, diag_tiles, near_tiles, att_heads)
        g_rows, m_cols, e_cols = _mlstm_gates(h.reshape(bsz, t, d), w_if_t, bias_row)
        y_ml = _mlstm(proj, conv_qk[l], g_rows, m_cols, e_cols, g_mlstm[l], 3 * att_w)
        y_sgu = _sgu(proj_uv, w_sgu[l], b_sgu[l], g_sgu[l], 0)
        y = _matmul3_stream(y_att.reshape(m, att_w), y_ml.reshape(m, ml_w), y_sgu.reshape(m, sgu_w),
                            w_out, l, 1024, 1024, bf16, "out_proj")
        x, h = _postnorm_prenorm(x, y.reshape(bsz, t, d), g_mix_post[l], base_mod, tab, 2,
                                 g_ffn_pre[l], tab, 3, 4)

        act, w_down_b = _swiglu_matmul(h.reshape(m, d), w_gate, w_up, w_down, l, 2048, 256)
        y = _matmul_rows(act, w_down_b, 512, 512, bf16, "ffn_down").reshape(bsz, t, d)
        if l + 1 < depth:
            x, h = _postnorm_prenorm(x, y, g_ffn_post[l], base_mod, tab, 5,
                                     g_mix_pre[l + 1], ada_table[l + 1], 0, 1)
        else:
            x = _postnorm_residual(x, y, g_ffn_post[l], base_mod, tab, 5)
    return x
```

```python
import functools
import math

import numpy as np
import jax
import jax.numpy as jnp
from jax import lax
from jax.experimental import pallas as pl
from jax.experimental.pallas import tpu as pltpu

ATT_HEAD_DIM = 128
MOBA_BLOCK = 256
MOBA_TOPK = 3
REL_BUCKETS = 32
REL_MAX_DIST = 128
MLSTM_HEADS = 4
MLSTM_BLOCK = 256
CONV_WIDTH = 4
SGU_GROUPS = 8
SGU_CHUNK = 128
N_MOD = 6
RMS_EPS = 1e-6
LN_EPS = 1e-5
MASKED = -1e30
LOG2E = math.log2(math.e)

VMEM_LIMIT_BYTES = 56 * 1024 * 1024


def _cparams(n_axes):
    return pltpu.CompilerParams(dimension_semantics=("arbitrary",) * n_axes,
                                vmem_limit_bytes=VMEM_LIMIT_BYTES)


def _dot(a, b):
    return jnp.dot(a, b, preferred_element_type=jnp.float32)


def _dot_nt(a, b):
    return lax.dot_general(a, b, (((1,), (1,)), ((), ())), preferred_element_type=jnp.float32)


def _silu(x):
    return x * jax.nn.sigmoid(x)


def _gelu_tanh(x):
    return 0.5 * x * (1.0 + jnp.tanh(math.sqrt(2.0 / math.pi) * (x + 0.044715 * (x * x * x))))


def _ada_kernel(c_ref, w_ref, b_ref, o_ref):
    a = _silu(c_ref[...]).astype(jnp.bfloat16)
    o_ref[...] = _dot(a, w_ref[...].astype(jnp.bfloat16)) + b_ref[...]


def _ada_base(c, w_ada, b_ada):
    b, d = c.shape
    n = w_ada.shape[1]
    rows = 16
    c_pad = jnp.zeros((rows, d), jnp.float32).at[:b].set(c)
    tn = 1024
    out = pl.pallas_call(
        _ada_kernel,
        grid=(n // tn,),
        in_specs=[pl.BlockSpec((rows, d), lambda j: (0, 0)),
                  pl.BlockSpec((d, tn), lambda j: (0, j)),
                  pl.BlockSpec((1, tn), lambda j: (0, j))],
        out_specs=pl.BlockSpec((rows, tn), lambda j: (0, j)),
        out_shape=jax.ShapeDtypeStruct((rows, n), jnp.float32),
        compiler_params=_cparams(1),
        name="ada_base",
    )(c_pad, w_ada, b_ada.reshape(1, n))
    return out[:b].reshape(b, N_MOD, n // N_MOD)


def _prenorm_kernel(shift_idx, scale_idx, x_ref, g_ref, mod_ref, tab_ref, o_ref):
    x = x_ref[0]
    y = x * lax.rsqrt(jnp.mean(x * x, axis=-1, keepdims=True) + RMS_EPS) * g_ref[...]
    scale = mod_ref[0, scale_idx:scale_idx + 1, :] + tab_ref[scale_idx:scale_idx + 1, :]
    shift = mod_ref[0, shift_idx:shift_idx + 1, :] + tab_ref[shift_idx:shift_idx + 1, :]
    o_ref[0] = (y * (1.0 + scale) + shift).astype(o_ref.dtype)


def _prenorm(x, g, base_mod, tab, shift_idx, scale_idx):
    b, t, d = x.shape
    tt = 256
    return pl.pallas_call(
        functools.partial(_prenorm_kernel, shift_idx, scale_idx),
        grid=(b, t // tt),
        in_specs=[pl.BlockSpec((1, tt, d), lambda bi, ti: (bi, ti, 0)),
                  pl.BlockSpec((1, d), lambda bi, ti: (0, 0)),
                  pl.BlockSpec((1, N_MOD, d), lambda bi, ti: (bi, 0, 0)),
                  pl.BlockSpec((N_MOD, d), lambda bi, ti: (0, 0))],
        out_specs=pl.BlockSpec((1, tt, d), lambda bi, ti: (bi, ti, 0)),
        out_shape=jax.ShapeDtypeStruct((b, t, d), jnp.bfloat16),
        compiler_params=_cparams(2),
        name="prenorm",
    )(x, g.reshape(1, d), base_mod, tab)


def _postnorm_kernel(gate_idx, x_ref, y_ref, g_ref, mod_ref, tab_ref, o_ref):
    y = y_ref[0].astype(jnp.float32)
    yn = y * lax.rsqrt(jnp.mean(y * y, axis=-1, keepdims=True) + RMS_EPS) * g_ref[...]
    gate = mod_ref[0, gate_idx:gate_idx + 1, :] + tab_ref[gate_idx:gate_idx + 1, :]
    o_ref[0] = x_ref[0] + gate * yn


def _post_pre_kernel(gate_idx, shift_idx, scale_idx, x_ref, y_ref, g_ref, gn_ref, mod_ref, tab_ref, tabn_ref,
                     o_ref, h_ref):
    _postnorm_kernel(gate_idx, x_ref, y_ref, g_ref, mod_ref, tab_ref, o_ref)
    _prenorm_kernel(shift_idx, scale_idx, o_ref, gn_ref, mod_ref, tabn_ref, h_ref)


def _postnorm_prenorm(x, y, g_post, base_mod, tab, gate_idx, g_pre, tab_next, shift_idx, scale_idx):
    b, t, d = x.shape
    tt = 256
    row_spec = pl.BlockSpec((1, tt, d), lambda bi, ti: (bi, ti, 0))
    vec_spec = pl.BlockSpec((1, d), lambda bi, ti: (0, 0))
    tab_spec = pl.BlockSpec((N_MOD, d), lambda bi, ti: (0, 0))
    return pl.pallas_call(
        functools.partial(_post_pre_kernel, gate_idx, shift_idx, scale_idx),
        grid=(b, t // tt),
        in_specs=[row_spec, row_spec, vec_spec, vec_spec,
                  pl.BlockSpec((1, N_MOD, d), lambda bi, ti: (bi, 0, 0)), tab_spec, tab_spec],
        out_specs=[row_spec, row_spec],
        out_shape=[jax.ShapeDtypeStruct((b, t, d), jnp.float32), jax.ShapeDtypeStruct((b, t, d), jnp.bfloat16)],
        compiler_params=_cparams(2),
        name="postnorm_prenorm",
    )(x, y, g_post.reshape(1, d), g_pre.reshape(1, d), base_mod, tab, tab_next)


def _postnorm_residual(x, y, g, base_mod, tab, gate_idx):
    b, t, d = x.shape
    tt = 256
    return pl.pallas_call(
        functools.partial(_postnorm_kernel, gate_idx),
        grid=(b, t // tt),
        in_specs=[pl.BlockSpec((1, tt, d), lambda bi, ti: (bi, ti, 0)),
                  pl.BlockSpec((1, tt, d), lambda bi, ti: (bi, ti, 0)),
                  pl.BlockSpec((1, d), lambda bi, ti: (0, 0)),
                  pl.BlockSpec((1, N_MOD, d), lambda bi, ti: (bi, 0, 0)),
                  pl.BlockSpec((N_MOD, d), lambda bi, ti: (0, 0))],
        out_specs=pl.BlockSpec((1, tt, d), lambda bi, ti: (bi, ti, 0)),
        out_shape=jax.ShapeDtypeStruct((b, t, d), jnp.float32),
        compiler_params=_cparams(2),
        name="postnorm_residual",
    )(x, y, g.reshape(1, d), base_mod, tab)


def _streamed_steps(w_refs, wb_refs, multiply):
    t, i = pl.program_id(0), pl.program_id(1)

    def stage(slot):
        for w_ref, wb_ref in zip(w_refs, wb_refs):
            rows = w_ref.shape[1]
            wb_ref[slot, pl.ds(pl.multiple_of(i * rows, rows), rows), :] = w_ref[0].astype(jnp.bfloat16)

    @pl.when(t == 0)
    def _():
        stage(0)

    for parity in (0, 1):
        @pl.when((t > 0) & (t % 2 == parity))
        def _():
            stage(parity)
            multiply(*[wb_ref[1 - parity] for wb_ref in wb_refs])


def _stream_specs(tm, n_tiles, x_widths, out_tn):
    row = lambda t, i: jnp.where(t > 0, i, 0)
    x_specs = [pl.BlockSpec((tm, kx), lambda t, i: (row(t, i), 0)) for kx in x_widths]
    out_spec = pl.BlockSpec((tm, out_tn), lambda t, i: (row(t, i), jnp.maximum(t - 1, 0)))
    return x_specs, out_spec


def _mm_stream_nt_kernel(x_ref, w_ref, o_ref, wb_ref):
    def multiply(w):
        o_ref[...] = _dot_nt(x_ref[...], w).astype(o_ref.dtype)

    _streamed_steps([w_ref], [wb_ref], multiply)


def _matmul_stream_nt(x, wt_stack, layer, n_cols, tm, tn, out_dtype, name):
    m, k = x.shape
    n_tiles, n_rows = n_cols // tn, m // tm
    piece = tn // n_rows
    x_specs, out_spec = _stream_specs(tm, n_tiles, [k], tn)
    return pl.pallas_call(
        _mm_stream_nt_kernel,
        grid=(n_tiles + 1, n_rows),
        in_specs=x_specs + [pl.BlockSpec(
            (1, piece, k), lambda t, i: (layer, jnp.minimum(t, n_tiles - 1) * n_rows + i, 0))],
        out_specs=out_spec,
        out_shape=jax.ShapeDtypeStruct((m, n_cols), out_dtype),
        scratch_shapes=[pltpu.VMEM((2, tn, k), jnp.bfloat16)],
        compiler_params=_cparams(2),
        name=name,
    )(x, wt_stack)


def _stream_weight_spec(w_stack, layer, tn, n_rows):
    k, n = w_stack.shape[1], w_stack.shape[2]
    n_tiles = n // tn
    return pl.BlockSpec((1, k // n_rows, tn), lambda t, i: (layer, i, jnp.minimum(t, n_tiles - 1)))


def _mm3_stream_kernel(xa_ref, xb_ref, xc_ref, w_ref, o_ref, wb_ref):
    def multiply(w):
        ka, kb = xa_ref.shape[1], xb_ref.shape[1]
        o_ref[...] = (_dot(xa_ref[...], w[0:ka, :]) + _dot(xb_ref[...], w[ka:ka + kb, :])
                      + _dot(xc_ref[...], w[ka + kb:, :])).astype(o_ref.dtype)

    _streamed_steps([w_ref], [wb_ref], multiply)


def _matmul3_stream(xa, xb, xc, w_stack, layer, tm, tn, out_dtype, name):
    m = xa.shape[0]
    k, n = w_stack.shape[1], w_stack.shape[2]
    n_tiles, n_rows = n // tn, m // tm
    x_specs, out_spec = _stream_specs(tm, n_tiles, [xa.shape[1], xb.shape[1], xc.shape[1]], tn)
    return pl.pallas_call(
        _mm3_stream_kernel,
        grid=(n_tiles + 1, n_rows),
        in_specs=x_specs + [_stream_weight_spec(w_stack, layer, tn, n_rows)],
        out_specs=out_spec,
        out_shape=jax.ShapeDtypeStruct((m, n), out_dtype),
        scratch_shapes=[pltpu.VMEM((2, k, tn), jnp.bfloat16)],
        compiler_params=_cparams(2),
        name=name,
    )(xa, xb, xc, w_stack)


def _swiglu_stream_kernel(x_ref, wg_ref, wu_ref, wd_ref, o_ref, wdb_ref, wgb_ref, wub_ref):
    def multiply(wg, wu):
        x = x_ref[...]
        o_ref[...] = (_silu(_dot(x, wg)) * _dot(x, wu)).astype(o_ref.dtype)
        wdb_ref[...] = wd_ref[0].astype(jnp.bfloat16)

    _streamed_steps([wg_ref, wu_ref], [wgb_ref, wub_ref], multiply)


def _swiglu_matmul(x, wg_stack, wu_stack, wd_stack, layer, tm, tn):
    m, k = x.shape
    n = wg_stack.shape[2]
    n_tiles, n_rows = n // tn, m // tm
    slab = wd_stack.shape[1] // (n_tiles * n_rows)
    assert slab * n_tiles * n_rows == wd_stack.shape[1]
    slab_index = lambda t, i: jnp.maximum((t - 1) * n_rows + i, 0)
    x_specs, out_spec = _stream_specs(tm, n_tiles, [k], tn)
    w_spec = _stream_weight_spec(wg_stack, layer, tn, n_rows)
    return pl.pallas_call(
        _swiglu_stream_kernel,
        grid=(n_tiles + 1, n_rows),
        in_specs=x_specs + [w_spec, w_spec,
                            pl.BlockSpec((1, slab, wd_stack.shape[2]), lambda t, i: (layer, slab_index(t, i), 0))],
        out_specs=[out_spec, pl.BlockSpec((slab, wd_stack.shape[2]), lambda t, i: (slab_index(t, i), 0))],
        out_shape=[jax.ShapeDtypeStruct((m, n), jnp.bfloat16),
                   jax.ShapeDtypeStruct(wd_stack.shape[1:], jnp.bfloat16)],
        scratch_shapes=[pltpu.VMEM((2, k, tn), jnp.bfloat16), pltpu.VMEM((2, k, tn), jnp.bfloat16)],
        compiler_params=_cparams(2),
        name="ffn_gate_up",
    )(x, wg_stack, wu_stack, wd_stack)


def _mm_rows_kernel(x_ref, w_ref, o_ref):
    o_ref[...] = _dot(x_ref[...], w_ref[...]).astype(o_ref.dtype)


def _matmul_rows(x, w, tm, tn, out_dtype, name):
    m, k = x.shape
    n = w.shape[1]
    return pl.pallas_call(
        _mm_rows_kernel,
        grid=(m // tm, n // tn),
        in_specs=[pl.BlockSpec((tm, k), lambda i, j: (i, 0)),
                  pl.BlockSpec((k, tn), lambda i, j: (0, j))],
        out_specs=pl.BlockSpec((tm, tn), lambda i, j: (i, j)),
        out_shape=jax.ShapeDtypeStruct((m, n), out_dtype),
        compiler_params=_cparams(2),
        name=name,
    )(x, w)


def _rel_bucket_starts():
    max_exact = REL_BUCKETS // 2
    d = np.arange(0, 4 * REL_MAX_DIST)
    ratio = np.maximum(d, 1).astype(np.float32) / np.float32(max_exact)
    large = max_exact + (np.log(ratio) / np.float32(math.log(REL_MAX_DIST / max_exact))
                         * np.float32(REL_BUCKETS - max_exact)).astype(np.int32)
    bucket = np.where(d < max_exact, d, np.minimum(large, REL_BUCKETS - 1))
    assert np.all(np.diff(bucket) >= 0)
    return [int(np.argmax(bucket == j)) for j in range(REL_BUCKETS)]


def _rel_tiles_kernel(starts, rel_ref, diag_ref, near_ref):
    h = pl.program_id(0)
    qry = lax.broadcasted_iota(jnp.int32, (MOBA_BLOCK, MOBA_BLOCK), 0)
    key = lax.broadcasted_iota(jnp.int32, (MOBA_BLOCK, MOBA_BLOCK), 1)
    dist = qry - key

    def bias_of(d):
        val = jnp.full(d.shape, rel_ref[0, h], jnp.float32)
        for j in range(1, REL_BUCKETS):
            val = jnp.where(d >= starts[j], rel_ref[j, h], val)
        return val

    diag_ref[0] = jnp.where(dist >= 0, bias_of(dist) * LOG2E, MASKED)
    near_ref[0] = bias_of(dist + MOBA_BLOCK) * LOG2E


def _rel_bias_tiles(rel_bias):
    heads = rel_bias.shape[1]
    tile = jax.ShapeDtypeStruct((heads, MOBA_BLOCK, MOBA_BLOCK), jnp.float32)
    spec = pl.BlockSpec((1, MOBA_BLOCK, MOBA_BLOCK), lambda h: (h, 0, 0))
    return pl.pallas_call(
        functools.partial(_rel_tiles_kernel, _rel_bucket_starts()),
        grid=(heads,),
        in_specs=[pl.BlockSpec(memory_space=pltpu.SMEM)],
        out_specs=[spec, spec],
        out_shape=[tile, tile],
        compiler_params=_cparams(1),
        name="rel_bias_tiles",
    )(rel_bias)


def _moba_kernel(n_blocks, rel_ref, q_ref, k_ref, v_ref, diag_ref, near_ref, o_ref,
                 kaug_ref, qaug_ref, s_ref):
    h = pl.program_id(1)
    blk, dh = MOBA_BLOCK, ATT_HEAD_DIM
    t = n_blocks * blk
    scale = dh ** -0.5 * LOG2E
    far_bias = rel_ref[REL_BUCKETS - 1, h] * LOG2E
    k = k_ref[0]

    row = lax.broadcasted_iota(jnp.int32, (t, dh), 0)
    col = lax.broadcasted_iota(jnp.int32, (t, dh), 1)
    lo = col * blk
    kaug_ref[:, 0:dh] = k
    kaug_ref[:, dh:2 * dh] = jnp.where((row >= lo) & (row < lo + blk), 1.0, 0.0).astype(jnp.bfloat16)

    brow = lax.broadcasted_iota(jnp.int32, (n_blocks, t), 0) * blk
    bcol = lax.broadcasted_iota(jnp.int32, (n_blocks, t), 1)
    avg = jnp.where((bcol >= brow) & (bcol < brow + blk), 1.0 / blk, 0.0).astype(jnp.bfloat16)
    kmean = _dot(avg, k)
    km_hi = kmean.astype(jnp.bfloat16)
    km_lo = (kmean - km_hi.astype(jnp.float32)).astype(jnp.bfloat16)

    q = q_ref[0]
    s_parts = _dot_nt(jnp.concatenate([km_hi, km_lo], axis=0), q)
    s_gate = s_parts[0:n_blocks, :] + s_parts[n_blocks:, :]
    bid = lax.broadcasted_iota(jnp.int32, s_gate.shape, 0)
    past = (bid + 1) * blk <= lax.broadcasted_iota(jnp.int32, s_gate.shape, 1)
    s_gate = jnp.where(past, s_gate, -jnp.inf)
    rank = jnp.zeros(s_gate.shape, jnp.int32)
    for j in range(n_blocks):
        s_j = s_gate[j:j + 1, :]
        beats = (s_j > s_gate) | ((s_j == s_gate) & (j < bid))
        rank = rank + jnp.where(beats, 1, 0)
    offs = jnp.where(past & (rank >= MOBA_TOPK), MASKED, 0.0)
    offs = jnp.concatenate([offs, jnp.zeros((dh - n_blocks, t), jnp.float32)], axis=0)
    qaug_ref[:, 0:dh] = q
    qaug_ref[:, dh:2 * dh] = offs.T.astype(jnp.bfloat16)

    def logits_of_key_block(j):
        cols = slice(j * blk, (j + 1) * blk)
        s_ref[j * blk:, cols] = _dot_nt(qaug_ref[j * blk:, :], kaug_ref[cols, :])

    def attend_query_block(i):
        rows = slice(i * blk, (i + 1) * blk)
        n_keys = (i + 1) * blk
        s = s_ref[rows, 0:n_keys] * scale
        parts = []
        if i >= 2:
            parts.append(s[:, :(i - 1) * blk] + far_bias)
        if i >= 1:
            parts.append(s[:, (i - 1) * blk:i * blk] + near_ref[0])
        parts.append(s[:, i * blk:] + diag_ref[0])
        s = jnp.concatenate(parts, axis=1) if len(parts) > 1 else parts[0]
        p = jnp.exp2(s - jnp.max(s, axis=1, keepdims=True))
        denom = jnp.sum(p, axis=1, keepdims=True)
        out = _dot(p.astype(jnp.bfloat16), v_ref[0, 0:n_keys, :]) / denom
        o_ref[0, rows, :] = out.astype(o_ref.dtype)

    logits_of_key_block(0)
    for i in range(n_blocks):
        if i + 1 < n_blocks:
            logits_of_key_block(i + 1)
        attend_query_block(i)


def _moba_attention(proj, rel_bias, diag_tiles, near_tiles, heads):
    b, t, _ = proj.shape
    blk, dh = MOBA_BLOCK, ATT_HEAD_DIM
    n_blocks = t // blk
    assert n_blocks <= dh
    tile_spec = pl.BlockSpec((1, blk, blk), lambda bi, h: (h, 0, 0))
    seq_spec = lambda off: pl.BlockSpec((1, t, dh), lambda bi, h: (bi, 0, off * heads + h))
    return pl.pallas_call(
        functools.partial(_moba_kernel, n_blocks),
        grid=(b, heads),
        in_specs=[pl.BlockSpec(memory_space=pltpu.SMEM), seq_spec(0), seq_spec(1), seq_spec(2),
                  tile_spec, tile_spec],
        out_specs=pl.BlockSpec((1, t, dh), lambda bi, h: (bi, 0, h)),
        out_shape=jax.ShapeDtypeStruct((b, t, heads * dh), jnp.bfloat16),
        scratch_shapes=[pltpu.VMEM((t, 2 * dh), jnp.bfloat16),
                        pltpu.VMEM((t, 2 * dh), jnp.bfloat16),
                        pltpu.VMEM((t, t), jnp.float32)],
        compiler_params=_cparams(2),
        name="moba_attention",
    )(rel_bias, proj, proj, proj, diag_tiles, near_tiles)


def _scan_rows(x, op, identity):
    n = x.shape[0]
    row = lax.broadcasted_iota(jnp.int32, x.shape, 0)
    s = 1
    while s < n:
        x = op(x, jnp.where(row >= s, pltpu.roll(x, s, axis=0), identity))
        s *= 2
    return x


def _mlstm_gates_kernel(h_ref, w_ref, bias_ref, g_ref, mb_ref, eb_ref):
    pre = _dot_nt(h_ref[0], w_ref[0].astype(jnp.bfloat16)) + bias_ref[...]
    lanes = pre.shape[1]
    i_pre = pre
    f_pre = pltpu.roll(pre, lanes - MLSTM_HEADS, axis=1)
    log_f = -(jnp.maximum(-f_pre, 0.0) + jnp.log1p(jnp.exp(-jnp.abs(f_pre))))
    b = _scan_rows(log_f, jnp.add, 0.0)
    g = i_pre - b
    m_rel = jnp.maximum(_scan_rows(g, jnp.maximum, -jnp.inf), 0.0)
    e = jnp.exp(-(b + m_rel))
    g_t = (g * LOG2E).T
    m_rel = m_rel * LOG2E
    t = pre.shape[0]
    for hh in range(MLSTM_HEADS):
        g_ref[0, hh] = g_t[hh:hh + 1, :]
        mb_ref[0, hh] = jnp.broadcast_to(m_rel[:, hh:hh + 1], (t, lanes))
        eb_ref[0, hh] = jnp.broadcast_to(e[:, hh:hh + 1], (t, lanes))


def _mlstm_gates(h, w_if_t, bias_row):
    b, t, d = h.shape
    lanes = w_if_t.shape[1]
    col = jax.ShapeDtypeStruct((b, MLSTM_HEADS, t, lanes), jnp.float32)
    col_spec = pl.BlockSpec((1, MLSTM_HEADS, t, lanes), lambda bi: (bi, 0, 0, 0))
    return pl.pallas_call(
        _mlstm_gates_kernel,
        grid=(b,),
        in_specs=[pl.BlockSpec((1, t, d), lambda bi: (bi, 0, 0)),
                  pl.BlockSpec((1, lanes, d), lambda bi: (0, 0, 0)),
                  pl.BlockSpec((1, lanes), lambda bi: (0, 0))],
        out_specs=[pl.BlockSpec((1, MLSTM_HEADS, 1, t), lambda bi: (bi, 0, 0, 0)), col_spec, col_spec],
        out_shape=[jax.ShapeDtypeStruct((b, MLSTM_HEADS, 1, t), jnp.float32), col, col],
        compiler_params=_cparams(1),
        name="mlstm_gates",
    )(h, w_if_t, bias_row)


def _causal_conv_silu(x, w):
    head_rows = 8
    row = lax.broadcasted_iota(jnp.int32, (head_rows, x.shape[1]), 0)
    out = x * w[CONV_WIDTH - 1:CONV_WIDTH, :]
    for back in range(1, CONV_WIDTH):
        rolled = pltpu.roll(x, back, axis=0)
        head = jnp.where(row >= back, rolled[0:head_rows, :], 0.0)
        shifted = jnp.concatenate([head, rolled[head_rows:, :]], axis=0)
        out = out + shifted * w[CONV_WIDTH - 1 - back:CONV_WIDTH - back, :]
    return _silu(out)


def _mlstm_kernel(q_ref, k_ref, v_ref, o_ref, wq_ref, wk_ref, g_ref, mb_ref, eb_ref, gn_ref, y_ref,
                  qc_ref, kc_ref):
    blk = MLSTM_BLOCK
    t, d = q_ref.shape[1], q_ref.shape[2]
    lanes = mb_ref.shape[3]
    qc_ref[...] = _causal_conv_silu(q_ref[0].astype(jnp.float32), wq_ref[...]).astype(jnp.bfloat16)
    kc = _causal_conv_silu(k_ref[0].astype(jnp.float32), wk_ref[...]) * (d ** -0.5)
    kc_ref[...] = kc.astype(jnp.bfloat16)
    qi = lax.broadcasted_iota(jnp.int32, (blk, blk), 0)
    ki = lax.broadcasted_iota(jnp.int32, (blk, blk), 1)

    for i in range(t // blk):
        rows = slice(i * blk, (i + 1) * blk)
        n_keys = (i + 1) * blk
        qk = _dot_nt(qc_ref[rows, :], kc_ref[0:n_keys, :])
        m_col = mb_ref[0, 0, rows, :]
        m_full = jnp.concatenate([m_col] * (n_keys // lanes), axis=1)
        w = jnp.exp2(g_ref[0, 0, :, 0:n_keys] - m_full)
        w_own = jnp.where(ki <= qi, w[:, i * blk:], 0.0)
        w = jnp.concatenate([w[:, :i * blk], w_own], axis=1) if i else w_own
        s = qk * w
        den = jnp.sum(s, axis=-1, keepdims=True)
        num = _dot(s.astype(jnp.bfloat16), v_ref[0, 0:n_keys, :])
        hid = num / jnp.maximum(jnp.abs(den), eb_ref[0, 0, rows, 0:1])
        hn = hid * lax.rsqrt(jnp.mean(hid * hid, axis=-1, keepdims=True) + RMS_EPS) * gn_ref[...]
        y_ref[0, rows, :] = (jax.nn.sigmoid(o_ref[0, rows, :].astype(jnp.float32)) * hn).astype(y_ref.dtype)


def _mlstm(proj, conv_w, g_rows, m_cols, e_cols, g_norm, col0):
    b, t, _ = proj.shape
    heads = MLSTM_HEADS
    d = g_norm.shape[0] // heads
    c0 = col0 // d
    lanes = m_cols.shape[3]
    full = lambda off: pl.BlockSpec((1, t, d), lambda bi, h: (bi, 0, c0 + off * heads + h))
    col_spec = pl.BlockSpec((1, 1, t, lanes), lambda bi, h: (bi, h, 0, 0))
    return pl.pallas_call(
        _mlstm_kernel,
        grid=(b, heads),
        in_specs=[full(0), full(1), full(2), full(3),
                  pl.BlockSpec((CONV_WIDTH, d), lambda bi, h: (0, h)),
                  pl.BlockSpec((CONV_WIDTH, d), lambda bi, h: (0, heads + h)),
                  pl.BlockSpec((1, 1, 1, t), lambda bi, h: (bi, h, 0, 0)),
                  col_spec, col_spec,
                  pl.BlockSpec((1, d), lambda bi, h: (0, h))],
        out_specs=pl.BlockSpec((1, t, d), lambda bi, h: (bi, 0, h)),
        out_shape=jax.ShapeDtypeStruct((b, t, heads * d), jnp.bfloat16),
        scratch_shapes=[pltpu.VMEM((t, d), jnp.bfloat16), pltpu.VMEM((t, d), jnp.bfloat16)],
        compiler_params=_cparams(2),
        name="mlstm",
    )(proj, proj, proj, proj, conv_w, conv_w, g_rows, m_cols, e_cols, g_norm.reshape(1, -1))


def _sgu_kernel(u_ref, v_ref, w_ref, bt_ref, g_ref, y_ref):
    chunk = SGU_CHUNK
    tt, width = u_ref.shape[1], u_ref.shape[2]
    gd = width // SGU_GROUPS
    v = _gelu_tanh(v_ref[0].astype(jnp.float32))
    vc = v - jnp.mean(v, axis=-1, keepdims=True)
    var = jnp.mean(vc * vc, axis=-1, keepdims=True)
    vn = (vc * lax.rsqrt(var + LN_EPS) * g_ref[...]).astype(jnp.bfloat16)
    row = lax.broadcasted_iota(jnp.int32, (chunk, chunk), 0)
    col = lax.broadcasted_iota(jnp.int32, (chunk, chunk), 1)
    for g in range(SGU_GROUPS):
        w = jnp.where(col <= row, w_ref[g], 0.0).astype(jnp.bfloat16)
        bias = bt_ref[:, g:g + 1]
        cols = slice(g * gd, (g + 1) * gd)
        for n in range(tt // chunk):
            rows = slice(n * chunk, (n + 1) * chunk)
            mixed = _dot(w, vn[rows, cols]) + bias
            u = _gelu_tanh(u_ref[0, rows, cols].astype(jnp.float32))
            y_ref[0, rows, cols] = (u * mixed).astype(y_ref.dtype)


def _sgu(proj, w_s, b_s, g_v, col0):
    b, t, _ = proj.shape
    width = g_v.shape[0]
    tt = 512
    c0 = col0 // width
    return pl.pallas_call(
        _sgu_kernel,
        grid=(b, t // tt),
        in_specs=[pl.BlockSpec((1, tt, width), lambda bi, ti: (bi, ti, c0)),
                  pl.BlockSpec((1, tt, width), lambda bi, ti: (bi, ti, c0 + 1)),
                  pl.BlockSpec(w_s.shape, lambda bi, ti: (0, 0, 0)),
                  pl.BlockSpec((SGU_CHUNK, SGU_GROUPS), lambda bi, ti: (0, 0)),
                  pl.BlockSpec((1, width), lambda bi, ti: (0, 0))],
        out_specs=pl.BlockSpec((1, tt, width), lambda bi, ti: (bi, ti, 0)),
        out_shape=jax.ShapeDtypeStruct((b, t, width), jnp.bfloat16),
        compiler_params=_cparams(2),
        name="sgu",
    )(proj, proj, w_s, b_s.T, g_v.reshape(1, width))


def kernel(x, c, w_ada, b_ada, ada_table, rel_bias, g_mix_pre, g_mix_post, g_ffn_pre, g_ffn_post,
           w_in, conv_qk, b_if, g_mlstm, w_sgu, b_sgu, g_sgu, w_out, w_gate, w_up, w_down):
    bsz, t, d = x.shape
    depth = w_in.shape[0]
    bf16 = jnp.bfloat16
    att_heads = rel_bias.shape[1]
    att_w = att_heads * ATT_HEAD_DIM
    ml_w = g_mlstm.shape[1]
    sgu_w = g_sgu.shape[1]
    n_gate = 2 * MLSTM_HEADS
    gate0 = 3 * att_w + 4 * ml_w
    m = bsz * t

    base_mod = _ada_base(c, w_ada, b_ada)
    diag_tiles, near_tiles = _rel_bias_tiles(rel_bias)
    w_in_t = jnp.swapaxes(w_in, 1, 2)

    h = _prenorm(x, g_mix_pre[0], base_mod, ada_table[0], 0, 1)
    for l in range(depth):
        tab = ada_table[l]
        w_if_t = jnp.pad(w_in_t[l:l + 1, gate0:gate0 + n_gate, :], ((0, 0), (0, 128 - n_gate), (0, 0)))
        w_uv_t = w_in_t[l:l + 1, gate0 + n_gate:, :]
        bias_row = jnp.pad(b_if[l].reshape(1, n_gate), ((0, 0), (0, 128 - n_gate)))

        h = h.reshape(m, d)
        proj = _matmul_stream_nt(h, w_in_t, l, gate0, 1024, 1024, bf16, "in_proj").reshape(bsz, t, gate0)
        proj_uv = _matmul_stream_nt(h, w_uv_t, 0, 2 * sgu_w, 1024, 1024, bf16, "in_proj_uv").reshape(bsz, t, 2 * sgu_w)

        y_att = _moba_attention(proj, rel_bias, diag_tiles, near_tiles, att_heads)
        g_rows, m_cols, e_cols = _mlstm_gates(h.reshape(bsz, t, d), w_if_t, bias_row)
        y_ml = _mlstm(proj, conv_qk[l], g_rows, m_cols, e_cols, g_mlstm[l], 3 * att_w)
        y_sgu = _sgu(proj_uv, w_sgu[l], b_sgu[l], g_sgu[l], 0)
        y = _matmul3_stream(y_att.reshape(m, att_w), y_ml.reshape(m, ml_w), y_sgu.reshape(m, sgu_w),
                            w_out, l, 1024, 1024, bf16, "out_proj")
        x, h = _postnorm_prenorm(x, y.reshape(bsz, t, d), g_mix_post[l], base_mod, tab, 2,
                                 g_ffn_pre[l], tab, 3, 4)

        act, w_down_b = _swiglu_matmul(h.reshape(m, d), w_gate, w_up, w_down, l, 2048, 256)
        y = _matmul_rows(act, w_down_b, 512, 512, bf16, "ffn_down").reshape(bsz, t, d)
        if l + 1 < depth:
            x, h = _postnorm_prenorm(x, y, g_ffn_post[l], base_mod, tab, 5,
                                     g_mix_pre[l + 1], ada_table[l + 1], 0, 1)
        else:
            x = _postnorm_residual(x, y, g_ffn_post[l], base_mod, tab, 5)
    return x
```
